```python
import jax, jax.numpy as jnp
from jax import lax
import numpy as np

D_MODEL = 1024
BATCH = 16
SEQ = 4096
DEPTH = 4

N_MIXERS = 3
N_POOL_LAYERS = (DEPTH + 2) // 3
N_MLA_LAYERS = (DEPTH + 1) // 3
N_CONV_LAYERS = DEPTH // 3

POOL_WINDOWS = (2, 4, 8, 16)
N_POOL_GROUPS = len(POOL_WINDOWS)
POOL_GROUP = D_MODEL // N_POOL_GROUPS

MLA_HEADS = D_MODEL // 64
QK_NOPE = 64
QK_ROPE = 32
V_HEAD = 64
Q_LORA = 3 * D_MODEL // 4
KV_LORA = D_MODEL // 4
ROPE_THETA = 10000.0
Q_BLOCK = 128

CONV_WIDTH = 3
FFN_HIDDEN = 2816

DEEPNORM_ALPHA = (2 * DEPTH) ** 0.25
DEEPNORM_BETA = (8 * DEPTH) ** -0.25
LN_EPS = 1e-5
RMS_EPS = 1e-6

kernel_name = 'hybrid_pool_mla_shortconv_deepnorm_adaln'


def layer_norm(x, g, b):
    xf = x.astype(jnp.float32)
    mu = jnp.mean(xf, axis=-1, keepdims=True)
    var = jnp.mean(jnp.square(xf - mu), axis=-1, keepdims=True)
    return ((xf - mu) * lax.rsqrt(var + LN_EPS) * g + b).astype(x.dtype)


def rms_norm(x, g):
    xf = x.astype(jnp.float32)
    y = xf * lax.rsqrt(jnp.mean(jnp.square(xf), axis=-1, keepdims=True) + RMS_EPS)
    return (y * g).astype(x.dtype)


def causal_dwconv(u, w):
    ch = u.shape[-1]
    return lax.conv_general_dilated(
        u, w[:, None, :].astype(u.dtype), window_strides=(1,), padding=[(CONV_WIDTH - 1, 0)],
        dimension_numbers=('NWC', 'WIO', 'NWC'), feature_group_count=ch)


def rope_tables(positions):
    inv_freq = ROPE_THETA ** (-jnp.arange(0, QK_ROPE, 2, dtype=jnp.float32) / QK_ROPE)
    ang = positions.astype(jnp.float32)[..., None] * inv_freq
    return jnp.cos(ang), jnp.sin(ang)


def apply_rope(x, cos, sin):
    half = x.shape[-1] // 2
    x1, x2 = x[..., :half], x[..., half:]
    cos = cos.astype(x.dtype)
    sin = sin.astype(x.dtype)
    return jnp.concatenate([x1 * cos - x2 * sin, x1 * sin + x2 * cos], axis=-1)


def pool_mixer(u, w_groups, scale):
    b, s, d = u.shape
    ug = u.reshape(b, s, N_POOL_GROUPS, POOL_GROUP)
    cs = jnp.cumsum(ug.astype(jnp.float32), axis=1)
    t = jnp.arange(s)
    means = []
    for g, w in enumerate(POOL_WINDOWS):
        csg = cs[:, :, g]
        prev = jnp.pad(csg, ((0, 0), (w, 0), (0, 0)))[:, :s]
        count = jnp.minimum(t + 1, w).astype(jnp.float32)[None, :, None]
        means.append((csg - prev) / count)
    pooled = jnp.stack(means, axis=2).astype(u.dtype) - ug
    y = jnp.einsum('bsgc,gcd->bsgd', pooled, w_groups).reshape(b, s, d)
    return y * scale


def mla_mixer(u, positions, w_a, q_norm, w_uq, kv_norm, w_ukv, w_o):
    b, s, _ = u.shape
    a = u @ w_a
    cq = rms_norm(a[..., :Q_LORA], q_norm)
    ckv = rms_norm(a[..., Q_LORA:Q_LORA + KV_LORA], kv_norm)
    k_pe = a[..., Q_LORA + KV_LORA:]
    q = (cq @ w_uq).reshape(b, s, MLA_HEADS, QK_NOPE + QK_ROPE)
    q_nope, q_pe = q[..., :QK_NOPE], q[..., QK_NOPE:]
    kv = (ckv @ w_ukv).reshape(b, s, MLA_HEADS, QK_NOPE + V_HEAD)
    k_nope, v = kv[..., :QK_NOPE], kv[..., QK_NOPE:]
    cos, sin = rope_tables(positions)
    q_pe = apply_rope(q_pe, cos[:, :, None], sin[:, :, None])
    k_pe = apply_rope(k_pe, cos, sin)
    sm_scale = (QK_NOPE + QK_ROPE) ** -0.5
    nb = s // Q_BLOCK
    qn_blocks = jnp.moveaxis(q_nope.reshape(b, nb, Q_BLOCK, MLA_HEADS, QK_NOPE), 1, 0)
    qp_blocks = jnp.moveaxis(q_pe.reshape(b, nb, Q_BLOCK, MLA_HEADS, QK_ROPE), 1, 0)
    key_idx = jnp.arange(s)
    neg = jnp.finfo(jnp.float32).min

    def attend(args):
        qn, qp, blk = args
        sc = (jnp.einsum('bqhd,bkhd->bhqk', qn, k_nope)
              + jnp.einsum('bqhr,bkr->bhqk', qp, k_pe)).astype(jnp.float32) * sm_scale
        q_idx = blk * Q_BLOCK + jnp.arange(Q_BLOCK)
        sc = jnp.where(key_idx[None, :] <= q_idx[:, None], sc, neg)
        p = jax.nn.softmax(sc, axis=-1).astype(v.dtype)
        return jnp.einsum('bhqk,bkhd->bqhd', p, v)

    o = lax.map(attend, (qn_blocks, qp_blocks, jnp.arange(nb)))
    o = jnp.moveaxis(o, 0, 1).reshape(b, s, MLA_HEADS * V_HEAD)
    return o @ w_o


def short_conv_mixer(u, w_in, conv_w, w_out):
    gb, gc, h = jnp.split(u @ w_in, 3, axis=-1)
    return (gb * causal_dwconv(gc * h, conv_w)) @ w_out


def conv_glu_ffn(u, w_up, conv_w, conv_b, w_down):
    h = causal_dwconv(u @ w_up, conv_w) + conv_b
    val, gate = jnp.split(h, 2, axis=-1)
    return (jax.nn.silu(gate) * val) @ w_down


def setup_inputs(seed: int = 0) -> dict:
    key = jax.random.key(seed)
    ks = iter(jax.random.split(key, 32))

    def nrm(shape, scale):
        return jax.random.normal(next(ks), shape, jnp.float32) * scale

    D, H, F = D_MODEL, MLA_HEADS, FFN_HIDDEN
    x = nrm((BATCH, SEQ, D), 1.0)
    c = nrm((BATCH, D), 1.0)
    offsets = jax.random.randint(next(ks), (BATCH, 1), 0, 1024)
    positions = (offsets + jnp.arange(SEQ)[None, :]).astype(jnp.int32)
    mod_w = nrm((DEPTH, D, 6 * D), 0.1 * D ** -0.5)
    mod_b = nrm((DEPTH, 6 * D), 0.02)
    ln_g = 1.0 + nrm((DEPTH, 2, D), 0.02)
    ln_b = nrm((DEPTH, 2, D), 0.02)
    pool_w = nrm((N_POOL_LAYERS, N_POOL_GROUPS, POOL_GROUP, POOL_GROUP), POOL_GROUP ** -0.5 * DEEPNORM_BETA)
    pool_scale = 1.0 + nrm((N_POOL_LAYERS, D), 0.1)
    mla_w_a = nrm((N_MLA_LAYERS, D, Q_LORA + KV_LORA + QK_ROPE), D ** -0.5)
    mla_q_norm = 1.0 + nrm((N_MLA_LAYERS, Q_LORA), 0.02)
    mla_w_uq = nrm((N_MLA_LAYERS, Q_LORA, H * (QK_NOPE + QK_ROPE)), Q_LORA ** -0.5)
    mla_kv_norm = 1.0 + nrm((N_MLA_LAYERS, KV_LORA), 0.02)
    mla_w_ukv = nrm((N_MLA_LAYERS, KV_LORA, H * (QK_NOPE + V_HEAD)), KV_LORA ** -0.5)
    mla_w_o = nrm((N_MLA_LAYERS, H * V_HEAD, D), (H * V_HEAD) ** -0.5 * DEEPNORM_BETA)
    sc_w_in = nrm((N_CONV_LAYERS, D, 3 * D), D ** -0.5)
    sc_conv = nrm((N_CONV_LAYERS, CONV_WIDTH, D), CONV_WIDTH ** -0.5)
    sc_w_out = nrm((N_CONV_LAYERS, D, D), D ** -0.5 * DEEPNORM_BETA)
    ffn_w_up = nrm((DEPTH, D, 2 * F), D ** -0.5)
    ffn_conv = nrm((DEPTH, CONV_WIDTH, 2 * F), CONV_WIDTH ** -0.5)
    ffn_conv_b = nrm((DEPTH, 2 * F), 0.02)
    ffn_w_down = nrm((DEPTH, F, D), F ** -0.5 * DEEPNORM_BETA)
    return {'x': x, 'c': c, 'positions': positions, 'mod_w': mod_w, 'mod_b': mod_b,
            'ln_g': ln_g, 'ln_b': ln_b, 'pool_w': pool_w, 'pool_scale': pool_scale,
            'mla_w_a': mla_w_a, 'mla_q_norm': mla_q_norm, 'mla_w_uq': mla_w_uq,
            'mla_kv_norm': mla_kv_norm, 'mla_w_ukv': mla_w_ukv, 'mla_w_o': mla_w_o,
            'sc_w_in': sc_w_in, 'sc_conv': sc_conv, 'sc_w_out': sc_w_out,
            'ffn_w_up': ffn_w_up, 'ffn_conv': ffn_conv, 'ffn_conv_b': ffn_conv_b, 'ffn_w_down': ffn_w_down}


def reference(x, c, positions, mod_w, mod_b, ln_g, ln_b, pool_w, pool_scale,
              mla_w_a, mla_q_norm, mla_w_uq, mla_kv_norm, mla_w_ukv, mla_w_o,
              sc_w_in, sc_conv, sc_w_out, ffn_w_up, ffn_conv, ffn_conv_b, ffn_w_down):
    cond = jax.nn.silu(c)
    for i in range(DEPTH):
        mod = (cond @ mod_w[i] + mod_b[i])[:, None, :]
        sh1, sc1, g1, sh2, sc2, g2 = jnp.split(mod, 6, axis=-1)
        u = x * (1.0 + sc1) + sh1
        kind, j = i % N_MIXERS, i // N_MIXERS
        if kind == 0:
            y = pool_mixer(u, pool_w[j], pool_scale[j])
        elif kind == 1:
            y = mla_mixer(u, positions, mla_w_a[j], mla_q_norm[j], mla_w_uq[j],
                          mla_kv_norm[j], mla_w_ukv[j], mla_w_o[j])
        else:
            y = short_conv_mixer(u, sc_w_in[j], sc_conv[j], sc_w_out[j])
        x = layer_norm(DEEPNORM_ALPHA * x + (1.0 + g1) * y, ln_g[i, 0], ln_b[i, 0])
        u = x * (1.0 + sc2) + sh2
        y = conv_glu_ffn(u, ffn_w_up[i], ffn_conv[i], ffn_conv_b[i], ffn_w_down[i])
        x = layer_norm(DEEPNORM_ALPHA * x + (1.0 + g2) * y, ln_g[i, 1], ln_b[i, 1])
    return x
```

```python
import functools

import numpy as np
import jax
import jax.numpy as jnp
from jax import lax
from jax.experimental import pallas as pl
from jax.experimental.pallas import tpu as pltpu

D_MODEL = 1024
DEPTH = 4
POOL_WINDOWS = (2, 4, 8, 16)
POOL_GROUP = D_MODEL // len(POOL_WINDOWS)
POOL_HALO = 16
MLA_HEADS = 16
HEAD_PAIRS = MLA_HEADS // 2
QK_NOPE = 64
QK_ROPE = 32
V_HEAD = 64
Q_LORA = 768
KV_LORA = 256
ROPE_THETA = 10000.0
FFN_HIDDEN = 2816
DEEPNORM_ALPHA = (2 * DEPTH) ** 0.25
LN_EPS = 1e-5
RMS_EPS = 1e-6
SM_SCALE = (QK_NOPE + QK_ROPE) ** -0.5

LANES = 128
SUBLANES = 8
HEAD_BLOCK = LANES
ROPE_LO = QK_NOPE
ROPE_HALF = QK_ROPE // 2

TM = 512
ROW_BLOCK = 64
FFN_CHUNK = 256
TQ = 256
MOD_TN = 1536
VMEM_LIMIT = 56 * 1024 * 1024

BF16 = jnp.bfloat16
F32 = jnp.float32


def _silu(v):
    return v * (1.0 / (1.0 + jnp.exp(-v)))


def _dot(a, b):
    return jnp.dot(a, b, preferred_element_type=F32)


def _resid_ln(x, y, g, ln_g, ln_b):
    r = DEEPNORM_ALPHA * x + (1.0 + g) * y
    mu = jnp.mean(r, axis=-1, keepdims=True)
    d = r - mu
    var = jnp.mean(d * d, axis=-1, keepdims=True)
    return d * lax.rsqrt(var + LN_EPS) * ln_g + ln_b


def _params(n_grid):
    return pltpu.CompilerParams(dimension_semantics=("arbitrary",) * n_grid,
                                vmem_limit_bytes=VMEM_LIMIT)


def _full(shape):
    n = len(shape)
    return pl.BlockSpec(shape, lambda *_: (0,) * n)


def _mod_kernel(c_ref, w_ref, b_ref, o_ref):
    cond = _silu(c_ref[...]).astype(BF16)
    o_ref[0] = _dot(cond, w_ref[0].astype(BF16)) + b_ref[0]


def _modulation(c, mod_w, mod_b):
    b = c.shape[0]
    n = mod_w.shape[-1]
    return pl.pallas_call(
        _mod_kernel,
        grid=(DEPTH, n // MOD_TN),
        in_specs=[_full((b, D_MODEL)),
                  pl.BlockSpec((1, D_MODEL, MOD_TN), lambda i, j: (i, 0, j)),
                  pl.BlockSpec((1, 1, MOD_TN), lambda i, j: (i, 0, j))],
        out_specs=pl.BlockSpec((1, b, MOD_TN), lambda i, j: (i, 0, j)),
        out_shape=jax.ShapeDtypeStruct((DEPTH, b, n), F32),
        compiler_params=_params(2),
        name="modulation",
    )(c, mod_w, mod_b.reshape(DEPTH, 1, n))


def _mlp_kernel(kind, row0, fc, nchunks, x_ref, mod_ref, wup_ref, cw_ref, cb_ref, wdn_ref, lng_ref, lnb_ref,
                o_ref, ubuf, pbuf, hbuf, abuf, carry, acc):
    wc = hbuf.shape[-1]

    @pl.when(pl.program_id(1) == 0)
    def _():
        carry[...] = jnp.zeros_like(carry)

    x = x_ref[0]
    sh = mod_ref[0, row0:row0 + 1, :]
    sc = mod_ref[0, row0 + 1:row0 + 2, :]
    ubuf[...] = (x * (1.0 + sc) + sh).astype(BF16)
    acc[...] = jnp.zeros_like(acc)

    def chunk(c, _):
        hbuf[0:SUBLANES, :] = carry[c]
        if kind == "ffn":
            hbuf[SUBLANES:SUBLANES + TM, :] = _dot(ubuf[...], wup_ref[c])
        else:
            pbuf[...] = _dot(ubuf[...], wup_ref[c])
            for r in range(0, TM, ROW_BLOCK):
                hbuf[SUBLANES + r:SUBLANES + r + ROW_BLOCK, :] = (
                    pbuf[r:r + ROW_BLOCK, fc:2 * fc] * pbuf[r:r + ROW_BLOCK, 2 * fc:3 * fc])
        carry[c] = hbuf[TM:TM + SUBLANES, :]
        cw = cw_ref[c]
        w0, w1, w2 = cw[0:1, :], cw[1:2, :], cw[2:3, :]
        for r in range(0, TM, ROW_BLOCK):
            conv = (w2 * hbuf[SUBLANES + r:SUBLANES + r + ROW_BLOCK, :]
                    + w1 * hbuf[SUBLANES - 1 + r:SUBLANES - 1 + r + ROW_BLOCK, :]
                    + w0 * hbuf[SUBLANES - 2 + r:SUBLANES - 2 + r + ROW_BLOCK, :])
            if kind == "ffn":
                conv = conv + cb_ref[c]
                a = _silu(conv[:, fc:]) * conv[:, :fc]
            else:
                a = pbuf[r:r + ROW_BLOCK, 0:fc] * conv
            abuf[r:r + ROW_BLOCK, :] = a.astype(BF16)
        acc[...] += _dot(abuf[...], wdn_ref[c])
        return 0

    lax.fori_loop(0, nchunks, chunk, 0)
    g = mod_ref[0, row0 + 2:row0 + 3, :]
    o_ref[0] = _resid_ln(x, acc[...], g, lng_ref[...], lnb_ref[...])


def _mlp_sublayer(kind, x, mod, row0, wup, cw, cb, wdn, ln_g, ln_b):
    b, s, d = x.shape
    nchunks, _, w = wup.shape
    fc = wdn.shape[1]
    wc = cw.shape[-1]
    kern = functools.partial(_mlp_kernel, kind, row0, fc, nchunks)
    return pl.pallas_call(
        kern,
        grid=(b, s // TM),
        in_specs=[pl.BlockSpec((1, TM, d), lambda i, j: (i, j, 0)),
                  pl.BlockSpec((1, 6, d), lambda i, j: (i, 0, 0)),
                  _full(wup.shape), _full(cw.shape), _full(cb.shape), _full(wdn.shape),
                  _full((1, d)), _full((1, d))],
        out_specs=pl.BlockSpec((1, TM, d), lambda i, j: (i, j, 0)),
        out_shape=jax.ShapeDtypeStruct(x.shape, F32),
        scratch_shapes=[pltpu.VMEM((TM, d), BF16),
                        pltpu.VMEM((TM, w), F32),
                        pltpu.VMEM((TM + SUBLANES, wc), F32),
                        pltpu.VMEM((TM, fc), BF16),
                        pltpu.VMEM((nchunks, SUBLANES, wc), F32),
                        pltpu.VMEM((TM, d), F32)],
        compiler_params=_params(2),
        name=kind + "_sublayer",
    )(x, mod, wup, cw, cb, wdn, ln_g.reshape(1, d), ln_b.reshape(1, d))


def _chunk_cols(w, parts, fc):
    lead = w.shape[:-1]
    n = w.shape[-1] // parts
    w = w.reshape(lead + (parts, n // fc, fc))
    w = jnp.moveaxis(w, -2, 0)
    return w.reshape((n // fc,) + lead + (parts * fc,))


def _ffn_sublayer(x, mod, w_up, conv_w, conv_b, w_down, ln_g, ln_b):
    fc = FFN_CHUNK
    wup = _chunk_cols(w_up.astype(BF16), 2, fc)
    cw = _chunk_cols(conv_w, 2, fc)
    cb = _chunk_cols(conv_b.reshape(1, -1), 2, fc)
    wdn = w_down.astype(BF16).reshape(FFN_HIDDEN // fc, fc, D_MODEL)
    return _mlp_sublayer("ffn", x, mod, 3, wup, cw, cb, wdn, ln_g, ln_b)


def _sconv_sublayer(x, mod, w_in, conv_w, w_out, ln_g, ln_b):
    fc = FFN_CHUNK
    wup = _chunk_cols(w_in.astype(BF16), 3, fc)
    cw = _chunk_cols(conv_w, 1, fc)
    cb = jnp.zeros((D_MODEL // fc, 1, fc), F32)
    wdn = w_out.astype(BF16).reshape(D_MODEL // fc, fc, D_MODEL)
    return _mlp_sublayer("sconv", x, mod, 0, wup, cw, cb, wdn, ln_g, ln_b)


def _pool_kernel(x_ref, mod_ref, w_ref, scale_ref, lng_ref, lnb_ref, o_ref, ubuf, ybuf):
    j = pl.program_id(1)

    @pl.when(j == 0)
    def _():
        ubuf[0:POOL_HALO, :] = jnp.zeros((POOL_HALO, D_MODEL), F32)

    @pl.when(j > 0)
    def _():
        ubuf[0:POOL_HALO, :] = ubuf[TM:TM + POOL_HALO, :]

    x = x_ref[0]
    sh = mod_ref[0, 0:1, :]
    sc = mod_ref[0, 1:2, :]
    ubuf[POOL_HALO:POOL_HALO + TM, :] = x * (1.0 + sc) + sh
    t = j * TM + lax.broadcasted_iota(jnp.int32, (TM, 1), 0)
    for gi, win in enumerate(POOL_WINDOWS):
        lo = gi * POOL_GROUP
        cols = slice(lo, lo + POOL_GROUP)
        u = ubuf[POOL_HALO:POOL_HALO + TM, cols]
        tot = u
        for k in range(1, win):
            tot = tot + ubuf[POOL_HALO - k:POOL_HALO - k + TM, cols]
        count = jnp.minimum(t + 1, win).astype(F32)
        pooled = tot / count - u
        ybuf[:, cols] = _dot(pooled.astype(BF16), w_ref[gi])
    y = ybuf[...] * scale_ref[...]
    g = mod_ref[0, 2:3, :]
    o_ref[0] = _resid_ln(x, y, g, lng_ref[...], lnb_ref[...])


def _pool_sublayer(x, mod, w_groups, scale, ln_g, ln_b):
    b, s, d = x.shape
    return pl.pallas_call(
        _pool_kernel,
        grid=(b, s // TM),
        in_specs=[pl.BlockSpec((1, TM, d), lambda i, j: (i, j, 0)),
                  pl.BlockSpec((1, 6, d), lambda i, j: (i, 0, 0)),
                  _full(w_groups.shape), _full((1, d)), _full((1, d)), _full((1, d))],
        out_specs=pl.BlockSpec((1, TM, d), lambda i, j: (i, j, 0)),
        out_shape=jax.ShapeDtypeStruct(x.shape, F32),
        scratch_shapes=[pltpu.VMEM((TM + POOL_HALO, d), F32),
                        pltpu.VMEM((TM, d), F32)],
        compiler_params=_params(2),
        name="pool_sublayer",
    )(x, mod, w_groups.astype(BF16), scale.reshape(1, d), ln_g.reshape(1, d), ln_b.reshape(1, d))


def _rope_tables():
    inv_freq = ROPE_THETA ** (-np.arange(0, QK_ROPE, 2, dtype=np.float32) / QK_ROPE)
    tab = np.zeros((SUBLANES, LANES), np.float32)
    a, m, e = ROPE_LO, ROPE_LO + ROPE_HALF, ROPE_LO + QK_ROPE
    tab[0, a:m] = inv_freq
    tab[0, m:e] = inv_freq
    tab[1, :a] = 1.0
    tab[2, a:e] = 1.0
    tab[3, a:m] = -1.0
    tab[4, m:e] = 1.0
    return jnp.asarray(tab)


def _rms(v, gain):
    return v * lax.rsqrt(jnp.mean(v * v, axis=-1, keepdims=True) + RMS_EPS) * gain


def _mla_proj_kernel(x_ref, mod_ref, pos_ref, tab_ref, waq_ref, wakv_ref, wape_ref, qn_ref, kvn_ref,
                     wuq_ref, wuk_ref, wuv_ref, q_ref, k_ref, v_ref):
    x = x_ref[0]
    sh = mod_ref[0, 0:1, :]
    sc = mod_ref[0, 1:2, :]
    u = (x * (1.0 + sc) + sh).astype(BF16)
    cq = _rms(_dot(u, waq_ref[...]), qn_ref[...]).astype(BF16)
    ckv = _rms(_dot(u, wakv_ref[...]), kvn_ref[...]).astype(BF16)
    ape = _dot(u, wape_ref[...])

    tab = tab_ref[...]
    ang = pos_ref[0] * tab[0:1, :]
    cs = jnp.cos(ang)
    sn = jnp.sin(ang)
    keep = tab[1:2, :] + tab[2:3, :] * cs
    s_lo = tab[3:4, :] * sn
    s_hi = tab[4:5, :] * sn

    def rope(v):
        return (v * keep + pltpu.roll(v, LANES - ROPE_HALF, 1) * s_lo
                + pltpu.roll(v, ROPE_HALF, 1) * s_hi)

    kpe = rope(ape)
    for hp in range(HEAD_PAIRS):
        q2 = _dot(cq, wuq_ref[hp])
        k2 = _dot(ckv, wuk_ref[hp])
        for hh in range(2):
            cols = slice(hh * HEAD_BLOCK, (hh + 1) * HEAD_BLOCK)
            q_ref[0, 2 * hp + hh] = (rope(q2[:, cols]) * SM_SCALE).astype(BF16)
            k_ref[0, 2 * hp + hh] = (k2[:, cols] + kpe).astype(BF16)
    v = _dot(ckv, wuv_ref[...])
    for hp in range(HEAD_PAIRS):
        v_ref[0, hp] = v[:, hp * LANES:(hp + 1) * LANES].astype(BF16)


def _mla_projections(x, mod, positions, w_a, q_norm, w_uq, kv_norm, w_ukv):
    b, s, d = x.shape
    w_a = w_a.astype(BF16)
    waq = w_a[:, :Q_LORA]
    wakv = w_a[:, Q_LORA:Q_LORA + KV_LORA]
    wape = jnp.zeros((d, HEAD_BLOCK), BF16).at[:, ROPE_LO:ROPE_LO + QK_ROPE].set(w_a[:, Q_LORA + KV_LORA:])
    wuq = w_uq.astype(BF16).reshape(Q_LORA, MLA_HEADS, QK_NOPE + QK_ROPE)
    wuq = jnp.pad(wuq, ((0, 0), (0, 0), (0, HEAD_BLOCK - QK_NOPE - QK_ROPE)))
    wuq = wuq.reshape(Q_LORA, HEAD_PAIRS, 2 * HEAD_BLOCK).transpose(1, 0, 2)
    wukv = w_ukv.astype(BF16).reshape(KV_LORA, MLA_HEADS, QK_NOPE + V_HEAD)
    wuk = jnp.pad(wukv[:, :, :QK_NOPE], ((0, 0), (0, 0), (0, HEAD_BLOCK - QK_NOPE)))
    wuk = wuk.reshape(KV_LORA, HEAD_PAIRS, 2 * HEAD_BLOCK).transpose(1, 0, 2)
    wuv = wukv[:, :, QK_NOPE:].reshape(KV_LORA, MLA_HEADS * V_HEAD)
    pos = positions.astype(F32).reshape(b, s, 1)
    head_shape = jax.ShapeDtypeStruct((b, MLA_HEADS, s, HEAD_BLOCK), BF16)
    return pl.pallas_call(
        _mla_proj_kernel,
        grid=(b, s // TM),
        in_specs=[pl.BlockSpec((1, TM, d), lambda i, j: (i, j, 0)),
                  pl.BlockSpec((1, 6, d), lambda i, j: (i, 0, 0)),
                  pl.BlockSpec((1, TM, 1), lambda i, j: (i, j, 0)),
                  _full((SUBLANES, LANES)),
                  _full(waq.shape), _full(wakv.shape), _full(wape.shape),
                  _full((1, Q_LORA)), _full((1, KV_LORA)),
                  _full(wuq.shape), _full(wuk.shape), _full(wuv.shape)],
        out_specs=[pl.BlockSpec((1, MLA_HEADS, TM, HEAD_BLOCK), lambda i, j: (i, 0, j, 0)),
                   pl.BlockSpec((1, MLA_HEADS, TM, HEAD_BLOCK), lambda i, j: (i, 0, j, 0)),
                   pl.BlockSpec((1, HEAD_PAIRS, TM, LANES), lambda i, j: (i, 0, j, 0))],
        out_shape=[head_shape, head_shape,
                   jax.ShapeDtypeStruct((b, HEAD_PAIRS, s, LANES), BF16)],
        compiler_params=_params(2),
        name="mla_projections",
    )(x, mod, pos, _rope_tables(), waq, wakv, wape, q_norm.reshape(1, -1), kv_norm.reshape(1, -1), wuq, wuk, wuv)


def _attn_kernel(q_ref, k_ref, v_ref, o_ref, m_sc, l_sc, acc_sc):
    qi = pl.program_id(2)
    row = lax.broadcasted_iota(jnp.int32, (TQ, TQ), 0)
    col = lax.broadcasted_iota(jnp.int32, (TQ, TQ), 1)
    for hh in range(2):
        q = q_ref[0, hh]
        m_sc[...] = jnp.full_like(m_sc, -1e30)
        l_sc[...] = jnp.zeros_like(l_sc)
        acc_sc[...] = jnp.zeros_like(acc_sc)

        def step(kb, masked):
            start = pl.multiple_of(kb * TQ, TQ)
            k = k_ref[0, hh, pl.ds(start, TQ), :]
            v = v_ref[0, 0, pl.ds(start, TQ), :]
            s = lax.dot_general(q, k, (((1,), (1,)), ((), ())), preferred_element_type=F32)
            if masked:
                s = jnp.where(col <= row, s, -1e30)
            m_prev = m_sc[...]
            m_new = jnp.maximum(m_prev, jnp.max(s, axis=-1, keepdims=True))
            p = jnp.exp(s - m_new)
            alpha = jnp.exp(m_prev - m_new)
            l_sc[...] = alpha * l_sc[...] + jnp.sum(p, axis=-1, keepdims=True)
            acc_sc[...] = alpha * acc_sc[...] + _dot(p.astype(BF16), v)
            m_sc[...] = m_new

        def body(kb, _):
            step(kb, False)
            return 0

        lax.fori_loop(0, qi, body, 0)
        step(qi, True)
        out = acc_sc[...] / l_sc[...]
        half = slice(hh * V_HEAD, (hh + 1) * V_HEAD)
        o_ref[0, :, half] = out[:, half].astype(BF16)


def _attention(q, k, v):
    b, _, s, _ = q.shape
    return pl.pallas_call(
        _attn_kernel,
        grid=(b, HEAD_PAIRS, s // TQ),
        in_specs=[pl.BlockSpec((1, 2, TQ, HEAD_BLOCK), lambda i, h, j: (i, h, j, 0)),
                  pl.BlockSpec((1, 2, s, HEAD_BLOCK), lambda i, h, j: (i, h, 0, 0)),
                  pl.BlockSpec((1, 1, s, LANES), lambda i, h, j: (i, h, 0, 0))],
        out_specs=pl.BlockSpec((1, TQ, LANES), lambda i, h, j: (i, j, h)),
        out_shape=jax.ShapeDtypeStruct((b, s, MLA_HEADS * V_HEAD), BF16),
        scratch_shapes=[pltpu.VMEM((TQ, 1), F32), pltpu.VMEM((TQ, 1), F32), pltpu.VMEM((TQ, LANES), F32)],
        compiler_params=_params(3),
        name="mla_attention",
    )(q, k, v)


def _out_proj_kernel(x_ref, a_ref, mod_ref, w_ref, lng_ref, lnb_ref, o_ref):
    y = _dot(a_ref[0], w_ref[...])
    g = mod_ref[0, 2:3, :]
    o_ref[0] = _resid_ln(x_ref[0], y, g, lng_ref[...], lnb_ref[...])


def _out_proj_sublayer(x, attn, mod, w_o, ln_g, ln_b):
    b, s, d = x.shape
    return pl.pallas_call(
        _out_proj_kernel,
        grid=(b, s // TM),
        in_specs=[pl.BlockSpec((1, TM, d), lambda i, j: (i, j, 0)),
                  pl.BlockSpec((1, TM, d), lambda i, j: (i, j, 0)),
                  pl.BlockSpec((1, 6, d), lambda i, j: (i, 0, 0)),
                  _full(w_o.shape), _full((1, d)), _full((1, d))],
        out_specs=pl.BlockSpec((1, TM, d), lambda i, j: (i, j, 0)),
        out_shape=jax.ShapeDtypeStruct(x.shape, F32),
        compiler_params=_params(2),
        name="mla_out_proj",
    )(x, attn, mod, w_o.astype(BF16), ln_g.reshape(1, d), ln_b.reshape(1, d))


def kernel(x, c, positions, mod_w, mod_b, ln_g, ln_b, pool_w, pool_scale, mla_w_a, mla_q_norm, mla_w_uq,
           mla_kv_norm, mla_w_ukv, mla_w_o, sc_w_in, sc_conv, sc_w_out, ffn_w_up, ffn_conv, ffn_conv_b, ffn_w_down):
    b = x.shape[0]
    mods = _modulation(c, mod_w, mod_b).reshape(DEPTH, b, 6, D_MODEL)
    for i in range(DEPTH):
        mod = mods[i]
        kind, j = i % 3, i // 3
        if kind == 0:
            x = _pool_sublayer(x, mod, pool_w[j], pool_scale[j], ln_g[i, 0], ln_b[i, 0])
        elif kind == 1:
            q, k, v = _mla_projections(x, mod, positions, mla_w_a[j], mla_q_norm[j], mla_w_uq[j],
                                       mla_kv_norm[j], mla_w_ukv[j])
            attn = _attention(q, k, v)
            x = _out_proj_sublayer(x, attn, mod, mla_w_o[j], ln_g[i, 0], ln_b[i, 0])
        else:
            x = _sconv_sublayer(x, mod, sc_w_in[j], sc_conv[j], sc_w_out[j], ln_g[i, 0], ln_b[i, 0])
        x = _ffn_sublayer(x, mod, ffn_w_up[i], ffn_conv[i], ffn_conv_b[i], ffn_w_down[i], ln_g[i, 1], ln_b[i, 1])
    return x
```

```python
import functools

import numpy as np
import jax
import jax.numpy as jnp
from jax import lax
from jax.experimental import pallas as pl
from jax.experimental.pallas import tpu as pltpu

D_MODEL = 1024
DEPTH = 4
POOL_WINDOWS = (2, 4, 8, 16)
POOL_GROUP = D_MODEL // len(POOL_WINDOWS)
POOL_HALO = 16
MLA_HEADS = 16
HEAD_PAIRS = MLA_HEADS // 2
QK_NOPE = 64
QK_ROPE = 32
V_HEAD = 64
Q_LORA = 768
KV_LORA = 256
ROPE_THETA = 10000.0
FFN_HIDDEN = 2816
DEEPNORM_ALPHA = (2 * DEPTH) ** 0.25
LN_EPS = 1e-5
RMS_EPS = 1e-6
SM_SCALE = (QK_NOPE + QK_ROPE) ** -0.5
LOG2E = 1.4426950408889634
NEG_BIG = -1e30

LANES = 128
SUBLANES = 8
HEAD_BLOCK = LANES
ROPE_LO = QK_NOPE
ROPE_HALF = QK_ROPE // 2

TM = 512
ROW_BLOCK = 64
FFN_CHUNK = 256
TQ = 512
MOD_TN = 1536
VMEM_LIMIT = 56 * 1024 * 1024

BF16 = jnp.bfloat16
F32 = jnp.float32


def _silu(v):
    return v * (1.0 / (1.0 + jnp.exp(-v)))


def _dot(a, b):
    return jnp.dot(a, b, preferred_element_type=F32)


def _resid_ln(x, y, g, ln_g, ln_b):
    r = DEEPNORM_ALPHA * x + (1.0 + g) * y
    mu = jnp.mean(r, axis=-1, keepdims=True)
    d = r - mu
    var = jnp.mean(d * d, axis=-1, keepdims=True)
    return d * lax.rsqrt(var + LN_EPS) * ln_g + ln_b


def _params(n_grid):
    return pltpu.CompilerParams(dimension_semantics=("arbitrary",) * n_grid,
                                vmem_limit_bytes=VMEM_LIMIT)


def _full(shape):
    n = len(shape)
    return pl.BlockSpec(shape, lambda *_: (0,) * n)


def _mod_kernel(c_ref, w_ref, b_ref, o_ref):
    cond = _silu(c_ref[...]).astype(BF16)
    o_ref[0] = _dot(cond, w_ref[0].astype(BF16)) + b_ref[0]


def _modulation(c, mod_w, mod_b):
    b = c.shape[0]
    n = mod_w.shape[-1]
    return pl.pallas_call(
        _mod_kernel,
        grid=(DEPTH, n // MOD_TN),
        in_specs=[_full((b, D_MODEL)),
                  pl.BlockSpec((1, D_MODEL, MOD_TN), lambda i, j: (i, 0, j)),
                  pl.BlockSpec((1, 1, MOD_TN), lambda i, j: (i, 0, j))],
        out_specs=pl.BlockSpec((1, b, MOD_TN), lambda i, j: (i, 0, j)),
        out_shape=jax.ShapeDtypeStruct((DEPTH, b, n), F32),
        compiler_params=_params(2),
        name="modulation",
    )(c, mod_w, mod_b.reshape(DEPTH, 1, n))


def _mlp_kernel(kind, row0, fc, nchunks, x_ref, mod_ref, wup_ref, cw_ref, cb_ref, wdn_ref, lng_ref, lnb_ref,
                o_ref, ubuf, pbuf, hbuf, abuf, carry, acc):
    @pl.when(pl.program_id(1) == 0)
    def _():
        carry[...] = jnp.zeros_like(carry)

    x = x_ref[0]
    sh = mod_ref[0, row0:row0 + 1, :]
    sc = mod_ref[0, row0 + 1:row0 + 2, :]
    ubuf[...] = (x * (1.0 + sc) + sh).astype(BF16)
    acc[...] = jnp.zeros_like(acc)

    def up(c, slot):
        hb = hbuf.at[slot]
        pb = pbuf.at[slot]
        hb[0:SUBLANES, :] = carry[c]
        if kind == "ffn":
            hb[SUBLANES:SUBLANES + TM, :] = _dot(ubuf[...], wup_ref[c])
        else:
            pb[...] = _dot(ubuf[...], wup_ref[c])
            for r in range(0, TM, ROW_BLOCK):
                hb[SUBLANES + r:SUBLANES + r + ROW_BLOCK, :] = (
                    pb[r:r + ROW_BLOCK, fc:2 * fc] * pb[r:r + ROW_BLOCK, 2 * fc:3 * fc])
        carry[c] = hb[TM:TM + SUBLANES, :]

    def finish(c, slot):
        hb = hbuf.at[slot]
        pb = pbuf.at[slot]
        ab = abuf.at[slot]
        cw = cw_ref[c]
        w0, w1, w2 = cw[0:1, :], cw[1:2, :], cw[2:3, :]
        for r in range(0, TM, ROW_BLOCK):
            conv = (w2 * hb[SUBLANES + r:SUBLANES + r + ROW_BLOCK, :]
                    + w1 * hb[SUBLANES - 1 + r:SUBLANES - 1 + r + ROW_BLOCK, :]
                    + w0 * hb[SUBLANES - 2 + r:SUBLANES - 2 + r + ROW_BLOCK, :])
            if kind == "ffn":
                conv = conv + cb_ref[c]
                a = _silu(conv[:, fc:]) * conv[:, :fc]
            else:
                a = pb[r:r + ROW_BLOCK, 0:fc] * conv
            ab[r:r + ROW_BLOCK, :] = a.astype(BF16)
        acc[...] += _dot(ab[...], wdn_ref[c])

    up(0, 0)
    for c in range(nchunks):
        if c + 1 < nchunks:
            up(c + 1, (c + 1) % 2)
        finish(c, c % 2)

    g = mod_ref[0, row0 + 2:row0 + 3, :]
    o_ref[0] = _resid_ln(x, acc[...], g, lng_ref[...], lnb_ref[...])


def _mlp_sublayer(kind, x, mod, row0, wup, cw, cb, wdn, ln_g, ln_b):
    b, s, d = x.shape
    nchunks, _, w = wup.shape
    fc = wdn.shape[1]
    wc = cw.shape[-1]
    kern = functools.partial(_mlp_kernel, kind, row0, fc, nchunks)
    pshape = (TM, w) if kind == "sconv" else (SUBLANES, LANES)
    return pl.pallas_call(
        kern,
        grid=(b, s // TM),
        in_specs=[pl.BlockSpec((1, TM, d), lambda i, j: (i, j, 0)),
                  pl.BlockSpec((1, 6, d), lambda i, j: (i, 0, 0)),
                  _full(wup.shape), _full(cw.shape), _full(cb.shape), _full(wdn.shape),
                  _full((1, d)), _full((1, d))],
        out_specs=pl.BlockSpec((1, TM, d), lambda i, j: (i, j, 0)),
        out_shape=jax.ShapeDtypeStruct(x.shape, F32),
        scratch_shapes=[pltpu.VMEM((TM, d), BF16),
                        pltpu.VMEM((2,) + pshape, F32),
                        pltpu.VMEM((2, TM + SUBLANES, wc), F32),
                        pltpu.VMEM((2, TM, fc), BF16),
                        pltpu.VMEM((nchunks, SUBLANES, wc), F32),
                        pltpu.VMEM((TM, d), F32)],
        compiler_params=_params(2),
        name=kind + "_sublayer",
    )(x, mod, wup, cw, cb, wdn, ln_g.reshape(1, d), ln_b.reshape(1, d))


def _chunk_cols(w, parts, fc):
    lead = w.shape[:-1]
    n = w.shape[-1] // parts
    w = w.reshape(lead + (parts, n // fc, fc))
    w = jnp.moveaxis(w, -2, 0)
    return w.reshape((n // fc,) + lead + (parts * fc,))


def _ffn_sublayer(x, mod, w_up, conv_w, conv_b, w_down, ln_g, ln_b):
    fc = FFN_CHUNK
    wup = _chunk_cols(w_up.astype(BF16), 2, fc)
    cw = _chunk_cols(conv_w, 2, fc)
    cb = _chunk_cols(conv_b.reshape(1, -1), 2, fc)
    wdn = w_down.astype(BF16).reshape(FFN_HIDDEN // fc, fc, D_MODEL)
    return _mlp_sublayer("ffn", x, mod, 3, wup, cw, cb, wdn, ln_g, ln_b)


def _sconv_sublayer(x, mod, w_in, conv_w, w_out, ln_g, ln_b):
    fc = FFN_CHUNK
    wup = _chunk_cols(w_in.astype(BF16), 3, fc)
    cw = _chunk_cols(conv_w, 1, fc)
    cb = jnp.zeros((D_MODEL // fc, 1, fc), F32)
    wdn = w_out.astype(BF16).reshape(D_MODEL // fc, fc, D_MODEL)
    return _mlp_sublayer("sconv", x, mod, 0, wup, cw, cb, wdn, ln_g, ln_b)


def _pool_kernel(x_ref, mod_ref, w_ref, scale_ref, lng_ref, lnb_ref, o_ref, ubuf, ybuf):
    j = pl.program_id(1)

    @pl.when(j == 0)
    def _():
        ubuf[0:POOL_HALO, :] = jnp.zeros((POOL_HALO, D_MODEL), F32)

    @pl.when(j > 0)
    def _():
        ubuf[0:POOL_HALO, :] = ubuf[TM:TM + POOL_HALO, :]

    x = x_ref[0]
    sh = mod_ref[0, 0:1, :]
    sc = mod_ref[0, 1:2, :]
    ubuf[POOL_HALO:POOL_HALO + TM, :] = x * (1.0 + sc) + sh
    t = j * TM + lax.broadcasted_iota(jnp.int32, (TM, 1), 0)
    for gi, win in enumerate(POOL_WINDOWS):
        lo = gi * POOL_GROUP
        cols = slice(lo, lo + POOL_GROUP)
        u = ubuf[POOL_HALO:POOL_HALO + TM, cols]
        tot = u
        for k in range(1, win):
            tot = tot + ubuf[POOL_HALO - k:POOL_HALO - k + TM, cols]
        count = jnp.minimum(t + 1, win).astype(F32)
        pooled = tot / count - u
        ybuf[:, cols] = _dot(pooled.astype(BF16), w_ref[gi])
    y = ybuf[...] * scale_ref[...]
    g = mod_ref[0, 2:3, :]
    o_ref[0] = _resid_ln(x, y, g, lng_ref[...], lnb_ref[...])


def _pool_sublayer(x, mod, w_groups, scale, ln_g, ln_b):
    b, s, d = x.shape
    return pl.pallas_call(
        _pool_kernel,
        grid=(b, s // TM),
        in_specs=[pl.BlockSpec((1, TM, d), lambda i, j: (i, j, 0)),
                  pl.BlockSpec((1, 6, d), lambda i, j: (i, 0, 0)),
                  _full(w_groups.shape), _full((1, d)), _full((1, d)), _full((1, d))],
        out_specs=pl.BlockSpec((1, TM, d), lambda i, j: (i, j, 0)),
        out_shape=jax.ShapeDtypeStruct(x.shape, F32),
        scratch_shapes=[pltpu.VMEM((TM + POOL_HALO, d), F32),
                        pltpu.VMEM((TM, d), F32)],
        compiler_params=_params(2),
        name="pool_sublayer",
    )(x, mod, w_groups.astype(BF16), scale.reshape(1, d), ln_g.reshape(1, d), ln_b.reshape(1, d))


def _rope_tables():
    inv_freq = ROPE_THETA ** (-np.arange(0, QK_ROPE, 2, dtype=np.float32) / QK_ROPE)
    tab = np.zeros((SUBLANES, LANES), np.float32)
    a, m, e = ROPE_LO, ROPE_LO + ROPE_HALF, ROPE_LO + QK_ROPE
    tab[0, a:m] = inv_freq
    tab[0, m:e] = inv_freq
    tab[1, :a] = 1.0
    tab[2, a:e] = 1.0
    tab[3, a:m] = -1.0
    tab[4, m:e] = 1.0
    return jnp.asarray(tab)


def _rms(v, gain):
    return v * lax.rsqrt(jnp.mean(v * v, axis=-1, keepdims=True) + RMS_EPS) * gain


def _mla_proj_kernel(x_ref, mod_ref, pos_ref, tab_ref, waq_ref, wakv_ref, wape_ref, qn_ref, kvn_ref,
                     wuq_ref, wuk_ref, wuv_ref, q_ref, k_ref, v_ref):
    x = x_ref[0]
    sh = mod_ref[0, 0:1, :]
    sc = mod_ref[0, 1:2, :]
    u = (x * (1.0 + sc) + sh).astype(BF16)
    cq = _rms(_dot(u, waq_ref[...]), qn_ref[...]).astype(BF16)
    ckv = _rms(_dot(u, wakv_ref[...]), kvn_ref[...]).astype(BF16)
    ape = _dot(u, wape_ref[...])

    tab = tab_ref[...]
    ang = pos_ref[0] * tab[0:1, :]
    cs = jnp.cos(ang)
    sn = jnp.sin(ang)
    keep = tab[1:2, :] + tab[2:3, :] * cs
    s_lo = tab[3:4, :] * sn
    s_hi = tab[4:5, :] * sn

    def rope(v):
        return (v * keep + pltpu.roll(v, LANES - ROPE_HALF, 1) * s_lo
                + pltpu.roll(v, ROPE_HALF, 1) * s_hi)

    kpe = rope(ape)
    for hp in range(HEAD_PAIRS):
        q2 = _dot(cq, wuq_ref[hp])
        k2 = _dot(ckv, wuk_ref[hp])
        for hh in range(2):
            cols = slice(hh * HEAD_BLOCK, (hh + 1) * HEAD_BLOCK)
            q_ref[0, 2 * hp + hh] = (rope(q2[:, cols]) * (SM_SCALE * LOG2E)).astype(BF16)
            k_ref[0, 2 * hp + hh] = (k2[:, cols] + kpe).astype(BF16)
    v = _dot(ckv, wuv_ref[...])
    for hp in range(HEAD_PAIRS):
        v_ref[0, hp] = v[:, hp * LANES:(hp + 1) * LANES].astype(BF16)


def _mla_projections(x, mod, positions, w_a, q_norm, w_uq, kv_norm, w_ukv):
    b, s, d = x.shape
    w_a = w_a.astype(BF16)
    waq = w_a[:, :Q_LORA]
    wakv = w_a[:, Q_LORA:Q_LORA + KV_LORA]
    wape = jnp.zeros((d, HEAD_BLOCK), BF16).at[:, ROPE_LO:ROPE_LO + QK_ROPE].set(w_a[:, Q_LORA + KV_LORA:])
    wuq = w_uq.astype(BF16).reshape(Q_LORA, MLA_HEADS, QK_NOPE + QK_ROPE)
    wuq = jnp.pad(wuq, ((0, 0), (0, 0), (0, HEAD_BLOCK - QK_NOPE - QK_ROPE)))
    wuq = wuq.reshape(Q_LORA, HEAD_PAIRS, 2 * HEAD_BLOCK).transpose(1, 0, 2)
    wukv = w_ukv.astype(BF16).reshape(KV_LORA, MLA_HEADS, QK_NOPE + V_HEAD)
    wuk = jnp.pad(wukv[:, :, :QK_NOPE], ((0, 0), (0, 0), (0, HEAD_BLOCK - QK_NOPE)))
    wuk = wuk.reshape(KV_LORA, HEAD_PAIRS, 2 * HEAD_BLOCK).transpose(1, 0, 2)
    wuv = wukv[:, :, QK_NOPE:].reshape(KV_LORA, MLA_HEADS * V_HEAD)
    pos = positions.astype(F32).reshape(b, s, 1)
    head_shape = jax.ShapeDtypeStruct((b, MLA_HEADS, s, HEAD_BLOCK), BF16)
    return pl.pallas_call(
        _mla_proj_kernel,
        grid=(b, s // TM),
        in_specs=[pl.BlockSpec((1, TM, d), lambda i, j: (i, j, 0)),
                  pl.BlockSpec((1, 6, d), lambda i, j: (i, 0, 0)),
                  pl.BlockSpec((1, TM, 1), lambda i, j: (i, j, 0)),
                  _full((SUBLANES, LANES)),
                  _full(waq.shape), _full(wakv.shape), _full(wape.shape),
                  _full((1, Q_LORA)), _full((1, KV_LORA)),
                  _full(wuq.shape), _full(wuk.shape), _full(wuv.shape)],
        out_specs=[pl.BlockSpec((1, MLA_HEADS, TM, HEAD_BLOCK), lambda i, j: (i, 0, j, 0)),
                   pl.BlockSpec((1, MLA_HEADS, TM, HEAD_BLOCK), lambda i, j: (i, 0, j, 0)),
                   pl.BlockSpec((1, HEAD_PAIRS, TM, LANES), lambda i, j: (i, 0, j, 0))],
        out_shape=[head_shape, head_shape,
                   jax.ShapeDtypeStruct((b, HEAD_PAIRS, s, LANES), BF16)],
        compiler_params=_params(2),
        name="mla_projections",
    )(x, mod, pos, _rope_tables(), waq, wakv, wape, q_norm.reshape(1, -1), kv_norm.reshape(1, -1), wuq, wuk, wuv)


def _attn_kernel(q_ref, k_ref, v_ref, o_ref, m_sc, l_sc, acc_sc):
    qi = pl.program_id(2)
    nrep = TQ // LANES
    m_sc[...] = jnp.full_like(m_sc, NEG_BIG)
    l_sc[...] = jnp.zeros_like(l_sc)
    acc_sc[...] = jnp.zeros_like(acc_sc)

    def step(kb, masked):
        start = pl.multiple_of(kb * TQ, TQ)
        v = v_ref[0, 0, pl.ds(start, TQ), :]
        scores = [lax.dot_general(q_ref[0, hh], k_ref[0, hh, pl.ds(start, TQ), :], (((1,), (1,)), ((), ())),
                                  preferred_element_type=F32) for hh in range(2)]
        for hh in range(2):
            s = scores[hh]
            if masked:
                row = lax.broadcasted_iota(jnp.int32, (TQ, TQ), 0)
                col = lax.broadcasted_iota(jnp.int32, (TQ, TQ), 1)
                s = jnp.where(col <= row, s, NEG_BIG)
            m_prev = m_sc[hh]
            m_new = jnp.maximum(m_prev, jnp.max(s, axis=-1, keepdims=True))
            p = jnp.exp2(s - pltpu.repeat(m_new, nrep, 1))
            alpha = jnp.exp2(m_prev - m_new)
            psum = p[:, 0:LANES]
            for t in range(1, nrep):
                psum = psum + p[:, t * LANES:(t + 1) * LANES]
            l_sc[hh] = alpha * l_sc[hh] + psum
            acc_sc[hh] = alpha * acc_sc[hh] + _dot(p.astype(BF16), v)
            m_sc[hh] = m_new

    def body(kb, _):
        step(kb, False)
        return 0

    lax.fori_loop(0, qi, body, 0)
    step(qi, True)
    for hh in range(2):
        out = acc_sc[hh] / jnp.sum(l_sc[hh], axis=-1, keepdims=True)
        half = slice(hh * V_HEAD, (hh + 1) * V_HEAD)
        o_ref[0, :, half] = out[:, half].astype(BF16)


def _attention(q, k, v):
    b, _, s, _ = q.shape
    return pl.pallas_call(
        _attn_kernel,
        grid=(b, HEAD_PAIRS, s // TQ),
        in_specs=[pl.BlockSpec((1, 2, TQ, HEAD_BLOCK), lambda i, h, j: (i, h, j, 0)),
                  pl.BlockSpec((1, 2, s, HEAD_BLOCK), lambda i, h, j: (i, h, 0, 0)),
                  pl.BlockSpec((1, 1, s, LANES), lambda i, h, j: (i, h, 0, 0))],
        out_specs=pl.BlockSpec((1, TQ, LANES), lambda i, h, j: (i, j, h)),
        out_shape=jax.ShapeDtypeStruct((b, s, MLA_HEADS * V_HEAD), BF16),
        scratch_shapes=[pltpu.VMEM((2, TQ, LANES), F32)] * 3,
        compiler_params=_params(3),
        name="mla_attention",
    )(q, k, v)


def _out_proj_kernel(x_ref, a_ref, mod_ref, w_ref, lng_ref, lnb_ref, o_ref):
    y = _dot(a_ref[0], w_ref[...])
    g = mod_ref[0, 2:3, :]
    o_ref[0] = _resid_ln(x_ref[0], y, g, lng_ref[...], lnb_ref[...])


def _out_proj_sublayer(x, attn, mod, w_o, ln_g, ln_b):
    b, s, d = x.shape
    return pl.pallas_call(
        _out_proj_kernel,
        grid=(b, s // TM),
        in_specs=[pl.BlockSpec((1, TM, d), lambda i, j: (i, j, 0)),
                  pl.BlockSpec((1, TM, d), lambda i, j: (i, j, 0)),
                  pl.BlockSpec((1, 6, d), lambda i, j: (i, 0, 0)),
                  _full(w_o.shape), _full((1, d)), _full((1, d))],
        out_specs=pl.BlockSpec((1, TM, d), lambda i, j: (i, j, 0)),
        out_shape=jax.ShapeDtypeStruct(x.shape, F32),
        compiler_params=_params(2),
        name="mla_out_proj",
    )(x, attn, mod, w_o.astype(BF16), ln_g.reshape(1, d), ln_b.reshape(1, d))


def kernel(x, c, positions, mod_w, mod_b, ln_g, ln_b, pool_w, pool_scale, mla_w_a, mla_q_norm, mla_w_uq,
           mla_kv_norm, mla_w_ukv, mla_w_o, sc_w_in, sc_conv, sc_w_out, ffn_w_up, ffn_conv, ffn_conv_b, ffn_w_down):
    b = x.shape[0]
    mods = _modulation(c, mod_w, mod_b).reshape(DEPTH, b, 6, D_MODEL)
    for i in range(DEPTH):
        mod = mods[i]
        kind, j = i % 3, i // 3
        if kind == 0:
            x = _pool_sublayer(x, mod, pool_w[j], pool_scale[j], ln_g[i, 0], ln_b[i, 0])
        elif kind == 1:
            q, k, v = _mla_projections(x, mod, positions, mla_w_a[j], mla_q_norm[j], mla_w_uq[j],
                                       mla_kv_norm[j], mla_w_ukv[j])
            attn = _attention(q, k, v)
            x = _out_proj_sublayer(x, attn, mod, mla_w_o[j], ln_g[i, 0], ln_b[i, 0])
        else:
            x = _sconv_sublayer(x, mod, sc_w_in[j], sc_conv[j], sc_w_out[j], ln_g[i, 0], ln_b[i, 0])
        x = _ffn_sublayer(x, mod, ffn_w_up[i], ffn_conv[i], ffn_conv_b[i], ffn_w_down[i], ln_g[i, 1], ln_b[i, 1])
    return x
```

```python
import functools

import numpy as np
import jax
import jax.numpy as jnp
from jax import lax
from jax.experimental import pallas as pl
from jax.experimental.pallas import tpu as pltpu

D_MODEL = 1024
DEPTH = 4
POOL_WINDOWS = (2, 4, 8, 16)
POOL_GROUP = D_MODEL // len(POOL_WINDOWS)
POOL_HALO = 16
MLA_HEADS = 16
HEAD_PAIRS = MLA_HEADS // 2
QK_NOPE = 64
QK_ROPE = 32
V_HEAD = 64
Q_LORA = 768
KV_LORA = 256
ROPE_THETA = 10000.0
FFN_HIDDEN = 2816
DEEPNORM_ALPHA = (2 * DEPTH) ** 0.25
LN_EPS = 1e-5
RMS_EPS = 1e-6
SM_SCALE = (QK_NOPE + QK_ROPE) ** -0.5
LOG2E = 1.4426950408889634
NEG_BIG = -1e30

LANES = 128
SUBLANES = 8
HEAD_BLOCK = LANES
ROPE_LO = QK_NOPE
ROPE_HALF = QK_ROPE // 2

TM = 512
ROW_BLOCK = 64
FFN_CHUNK = 256
DOWN_GROUP = 4
TQ = 512
MOD_TN = 1536
VMEM_LIMIT = 56 * 1024 * 1024

BF16 = jnp.bfloat16
F32 = jnp.float32


def _silu(v):
    return v * (1.0 / (1.0 + jnp.exp(-v)))


def _dot(a, b):
    return jnp.dot(a, b, preferred_element_type=F32)


def _resid_ln(x, y, g, ln_g, ln_b):
    r = DEEPNORM_ALPHA * x + (1.0 + g) * y
    mu = jnp.mean(r, axis=-1, keepdims=True)
    d = r - mu
    var = jnp.mean(d * d, axis=-1, keepdims=True)
    return d * lax.rsqrt(var + LN_EPS) * ln_g + ln_b


def _params(n_grid):
    return pltpu.CompilerParams(dimension_semantics=("arbitrary",) * n_grid,
                                vmem_limit_bytes=VMEM_LIMIT)


def _full(shape):
    n = len(shape)
    return pl.BlockSpec(shape, lambda *_: (0,) * n)


def _mod_kernel(c_ref, w_ref, b_ref, o_ref):
    cond = _silu(c_ref[...]).astype(BF16)
    o_ref[0] = _dot(cond, w_ref[0].astype(BF16)) + b_ref[0]


def _modulation(c, mod_w, mod_b):
    b = c.shape[0]
    n = mod_w.shape[-1]
    return pl.pallas_call(
        _mod_kernel,
        grid=(DEPTH, n // MOD_TN),
        in_specs=[_full((b, D_MODEL)),
                  pl.BlockSpec((1, D_MODEL, MOD_TN), lambda i, j: (i, 0, j)),
                  pl.BlockSpec((1, 1, MOD_TN), lambda i, j: (i, 0, j))],
        out_specs=pl.BlockSpec((1, b, MOD_TN), lambda i, j: (i, 0, j)),
        out_shape=jax.ShapeDtypeStruct((DEPTH, b, n), F32),
        compiler_params=_params(2),
        name="modulation",
    )(c, mod_w, mod_b.reshape(DEPTH, 1, n))


def _mlp_kernel(kind, row0, fc, nchunks, x_ref, mod_ref, wup_ref, cw_ref, cb_ref, wdn_ref, lng_ref, lnb_ref,
                o_ref, ubuf, pbuf, hbuf, abuf, carry, acc):
    @pl.when(pl.program_id(1) == 0)
    def _():
        carry[...] = jnp.zeros_like(carry)

    x = x_ref[0]
    sh = mod_ref[0, row0:row0 + 1, :]
    sc = mod_ref[0, row0 + 1:row0 + 2, :]
    ubuf[...] = (x * (1.0 + sc) + sh).astype(BF16)

    def up(c, slot):
        hb = hbuf.at[slot]
        pb = pbuf.at[slot]
        hb[0:SUBLANES, :] = carry[c]
        parts = 2 if kind == "ffn" else 3
        n = wup_ref.shape[1] // parts
        for p in range(parts):
            res = _dot(ubuf[...], wup_ref[:, p * n + c * fc:p * n + (c + 1) * fc])
            if kind == "ffn":
                hb[SUBLANES:SUBLANES + TM, p * fc:(p + 1) * fc] = res
            else:
                pb[:, p * fc:(p + 1) * fc] = res
        if kind == "sconv":
            for r in range(0, TM, ROW_BLOCK):
                hb[SUBLANES + r:SUBLANES + r + ROW_BLOCK, :] = (
                    pb[r:r + ROW_BLOCK, fc:2 * fc] * pb[r:r + ROW_BLOCK, 2 * fc:3 * fc])
        carry[c] = hb[TM:TM + SUBLANES, :]

    def activate(c, slot):
        hb = hbuf.at[slot]
        pb = pbuf.at[slot]
        cw = cw_ref[c]
        cb = cb_ref[c]

        def conv3(r, cols):
            return (cw[2:3, cols] * hb[SUBLANES + r:SUBLANES + r + ROW_BLOCK, cols]
                    + cw[1:2, cols] * hb[SUBLANES - 1 + r:SUBLANES - 1 + r + ROW_BLOCK, cols]
                    + cw[0:1, cols] * hb[SUBLANES - 2 + r:SUBLANES - 2 + r + ROW_BLOCK, cols])

        for lo in range(0, fc, LANES):
            val = slice(lo, lo + LANES)
            gate = slice(fc + lo, fc + lo + LANES)
            for r in range(0, TM, ROW_BLOCK):
                if kind == "ffn":
                    a = _silu(conv3(r, gate) + cb[:, gate]) * (conv3(r, val) + cb[:, val])
                else:
                    a = pb[r:r + ROW_BLOCK, val] * conv3(r, val)
                abuf[r:r + ROW_BLOCK, c * fc + lo:c * fc + lo + LANES] = a.astype(BF16)

    def down(c0, c1):
        return _dot(abuf[:, c0 * fc:c1 * fc], wdn_ref[c0 * fc:c1 * fc, :])

    y = None
    up(0, 0)
    for c in range(nchunks):
        if c + 1 < nchunks:
            up(c + 1, (c + 1) % 2)
        activate(c, c % 2)
        c0 = c - c % DOWN_GROUP
        if c + 1 == nchunks:
            y = down(c0, c + 1) if c0 == 0 else acc[...] + down(c0, c + 1)
        elif (c + 1) % DOWN_GROUP == 0:
            if c0 == 0:
                acc[...] = down(c0, c + 1)
            else:
                acc[...] += down(c0, c + 1)

    g = mod_ref[0, row0 + 2:row0 + 3, :]
    o_ref[0] = _resid_ln(x, y, g, lng_ref[...], lnb_ref[...])


def _mlp_sublayer(kind, x, mod, row0, wup, cw, cb, wdn, ln_g, ln_b):
    b, s, d = x.shape
    nchunks = cw.shape[0]
    fc = wdn.shape[0] // nchunks
    w = wup.shape[1] // nchunks
    wc = cw.shape[-1]
    kern = functools.partial(_mlp_kernel, kind, row0, fc, nchunks)
    pshape = (TM, w) if kind == "sconv" else (SUBLANES, LANES)
    return pl.pallas_call(
        kern,
        grid=(b, s // TM),
        in_specs=[pl.BlockSpec((1, TM, d), lambda i, j: (i, j, 0)),
                  pl.BlockSpec((1, 6, d), lambda i, j: (i, 0, 0)),
                  _full(wup.shape), _full(cw.shape), _full(cb.shape), _full(wdn.shape),
                  _full((1, d)), _full((1, d))],
        out_specs=pl.BlockSpec((1, TM, d), lambda i, j: (i, j, 0)),
        out_shape=jax.ShapeDtypeStruct(x.shape, F32),
        scratch_shapes=[pltpu.VMEM((TM, d), BF16),
                        pltpu.VMEM((2,) + pshape, F32),
                        pltpu.VMEM((2, TM + SUBLANES, wc), F32),
                        pltpu.VMEM((TM, nchunks * fc), BF16),
                        pltpu.VMEM((nchunks, SUBLANES, wc), F32),
                        pltpu.VMEM((TM, d), F32)],
        compiler_params=_params(2),
        name=kind + "_sublayer",
    )(x, mod, wup, cw, cb, wdn, ln_g.reshape(1, d), ln_b.reshape(1, d))


def _chunk_cols(w, parts, fc):
    lead = w.shape[:-1]
    n = w.shape[-1] // parts
    w = w.reshape(lead + (parts, n // fc, fc))
    w = jnp.moveaxis(w, -2, 0)
    return w.reshape((n // fc,) + lead + (parts * fc,))


def _ffn_sublayer(x, mod, w_up, conv_w, conv_b, w_down, ln_g, ln_b):
    fc = FFN_CHUNK
    wup = w_up.astype(BF16)
    cw = _chunk_cols(conv_w, 2, fc)
    cb = _chunk_cols(conv_b.reshape(1, -1), 2, fc)
    wdn = w_down.astype(BF16)
    return _mlp_sublayer("ffn", x, mod, 3, wup, cw, cb, wdn, ln_g, ln_b)


def _sconv_sublayer(x, mod, w_in, conv_w, w_out, ln_g, ln_b):
    fc = FFN_CHUNK
    wup = w_in.astype(BF16)
    cw = _chunk_cols(conv_w, 1, fc)
    cb = jnp.zeros((D_MODEL // fc, 1, fc), F32)
    wdn = w_out.astype(BF16)
    return _mlp_sublayer("sconv", x, mod, 0, wup, cw, cb, wdn, ln_g, ln_b)


def _pool_kernel(x_ref, mod_ref, w_ref, scale_ref, lng_ref, lnb_ref, o_ref, ubuf, ybuf):
    j = pl.program_id(1)

    @pl.when(j == 0)
    def _():
        ubuf[0:POOL_HALO, :] = jnp.zeros((POOL_HALO, D_MODEL), F32)

    @pl.when(j > 0)
    def _():
        ubuf[0:POOL_HALO, :] = ubuf[TM:TM + POOL_HALO, :]

    x = x_ref[0]
    sh = mod_ref[0, 0:1, :]
    sc = mod_ref[0, 1:2, :]
    ubuf[POOL_HALO:POOL_HALO + TM, :] = x * (1.0 + sc) + sh
    t = j * TM + lax.broadcasted_iota(jnp.int32, (TM, 1), 0)
    for gi, win in enumerate(POOL_WINDOWS):
        lo = gi * POOL_GROUP
        cols = slice(lo, lo + POOL_GROUP)
        u = ubuf[POOL_HALO:POOL_HALO + TM, cols]
        tot = u
        for k in range(1, win):
            tot = tot + ubuf[POOL_HALO - k:POOL_HALO - k + TM, cols]
        count = jnp.minimum(t + 1, win).astype(F32)
        pooled = tot / count - u
        ybuf[:, cols] = _dot(pooled.astype(BF16), w_ref[gi])
    y = ybuf[...] * scale_ref[...]
    g = mod_ref[0, 2:3, :]
    o_ref[0] = _resid_ln(x, y, g, lng_ref[...], lnb_ref[...])


def _pool_sublayer(x, mod, w_groups, scale, ln_g, ln_b):
    b, s, d = x.shape
    return pl.pallas_call(
        _pool_kernel,
        grid=(b, s // TM),
        in_specs=[pl.BlockSpec((1, TM, d), lambda i, j: (i, j, 0)),
                  pl.BlockSpec((1, 6, d), lambda i, j: (i, 0, 0)),
                  _full(w_groups.shape), _full((1, d)), _full((1, d)), _full((1, d))],
        out_specs=pl.BlockSpec((1, TM, d), lambda i, j: (i, j, 0)),
        out_shape=jax.ShapeDtypeStruct(x.shape, F32),
        scratch_shapes=[pltpu.VMEM((TM + POOL_HALO, d), F32),
                        pltpu.VMEM((TM, d), F32)],
        compiler_params=_params(2),
        name="pool_sublayer",
    )(x, mod, w_groups.astype(BF16), scale.reshape(1, d), ln_g.reshape(1, d), ln_b.reshape(1, d))


def _rope_tables():
    inv_freq = ROPE_THETA ** (-np.arange(0, QK_ROPE, 2, dtype=np.float32) / QK_ROPE)
    tab = np.zeros((SUBLANES, LANES), np.float32)
    a, m, e = ROPE_LO, ROPE_LO + ROPE_HALF, ROPE_LO + QK_ROPE
    tab[0, a:m] = inv_freq
    tab[0, m:e] = inv_freq
    tab[1, :a] = 1.0
    tab[2, a:e] = 1.0
    tab[3, a:m] = -1.0
    tab[4, m:e] = 1.0
    return jnp.asarray(tab)


def _rms(v, gain):
    return v * lax.rsqrt(jnp.mean(v * v, axis=-1, keepdims=True) + RMS_EPS) * gain


def _mla_proj_kernel(x_ref, mod_ref, pos_ref, tab_ref, waq_ref, wakv_ref, wape_ref, qn_ref, kvn_ref,
                     wuq_ref, wuk_ref, wuv_ref, q_ref, k_ref, v_ref):
    x = x_ref[0]
    sh = mod_ref[0, 0:1, :]
    sc = mod_ref[0, 1:2, :]
    u = (x * (1.0 + sc) + sh).astype(BF16)
    cq = _rms(_dot(u, waq_ref[...]), qn_ref[...]).astype(BF16)
    ckv = _rms(_dot(u, wakv_ref[...]), kvn_ref[...]).astype(BF16)
    ape = _dot(u, wape_ref[...])

    tab = tab_ref[...]
    ang = pos_ref[0] * tab[0:1, :]
    cs = jnp.cos(ang)
    sn = jnp.sin(ang)
    keep = tab[1:2, :] + tab[2:3, :] * cs
    s_lo = tab[3:4, :] * sn
    s_hi = tab[4:5, :] * sn

    def rope(v):
        return (v * keep + pltpu.roll(v, LANES - ROPE_HALF, 1) * s_lo
                + pltpu.roll(v, ROPE_HALF, 1) * s_hi)

    kpe = rope(ape)
    for hp in range(HEAD_PAIRS):
        q2 = _dot(cq, wuq_ref[hp])
        k2 = _dot(ckv, wuk_ref[hp])
        for hh in range(2):
            cols = slice(hh * HEAD_BLOCK, (hh + 1) * HEAD_BLOCK)
            q_ref[0, 2 * hp + hh] = (rope(q2[:, cols]) * (SM_SCALE * LOG2E)).astype(BF16)
            k_ref[0, 2 * hp + hh] = (k2[:, cols] + kpe).astype(BF16)
    v = _dot(ckv, wuv_ref[...])
    for hp in range(HEAD_PAIRS):
        v_ref[0, hp] = v[:, hp * LANES:(hp + 1) * LANES].astype(BF16)


def _mla_projections(x, mod, positions, w_a, q_norm, w_uq, kv_norm, w_ukv):
    b, s, d = x.shape
    w_a = w_a.astype(BF16)
    waq = w_a[:, :Q_LORA]
    wakv = w_a[:, Q_LORA:Q_LORA + KV_LORA]
    wape = jnp.zeros((d, HEAD_BLOCK), BF16).at[:, ROPE_LO:ROPE_LO + QK_ROPE].set(w_a[:, Q_LORA + KV_LORA:])
    wuq = w_uq.astype(BF16).reshape(Q_LORA, MLA_HEADS, QK_NOPE + QK_ROPE)
    wuq = jnp.pad(wuq, ((0, 0), (0, 0), (0, HEAD_BLOCK - QK_NOPE - QK_ROPE)))
    wuq = wuq.reshape(Q_LORA, HEAD_PAIRS, 2 * HEAD_BLOCK).transpose(1, 0, 2)
    wukv = w_ukv.astype(BF16).reshape(KV_LORA, MLA_HEADS, QK_NOPE + V_HEAD)
    wuk = jnp.pad(wukv[:, :, :QK_NOPE], ((0, 0), (0, 0), (0, HEAD_BLOCK - QK_NOPE)))
    wuk = wuk.reshape(KV_LORA, HEAD_PAIRS, 2 * HEAD_BLOCK).transpose(1, 0, 2)
    wuv = wukv[:, :, QK_NOPE:].reshape(KV_LORA, MLA_HEADS * V_HEAD)
    pos = positions.astype(F32).reshape(b, s, 1)
    head_shape = jax.ShapeDtypeStruct((b, MLA_HEADS, s, HEAD_BLOCK), BF16)
    return pl.pallas_call(
        _mla_proj_kernel,
        grid=(b, s // TM),
        in_specs=[pl.BlockSpec((1, TM, d), lambda i, j: (i, j, 0)),
                  pl.BlockSpec((1, 6, d), lambda i, j: (i, 0, 0)),
                  pl.BlockSpec((1, TM, 1), lambda i, j: (i, j, 0)),
                  _full((SUBLANES, LANES)),
                  _full(waq.shape), _full(wakv.shape), _full(wape.shape),
                  _full((1, Q_LORA)), _full((1, KV_LORA)),
                  _full(wuq.shape), _full(wuk.shape), _full(wuv.shape)],
        out_specs=[pl.BlockSpec((1, MLA_HEADS, TM, HEAD_BLOCK), lambda i, j: (i, 0, j, 0)),
                   pl.BlockSpec((1, MLA_HEADS, TM, HEAD_BLOCK), lambda i, j: (i, 0, j, 0)),
                   pl.BlockSpec((1, HEAD_PAIRS, TM, LANES), lambda i, j: (i, 0, j, 0))],
        out_shape=[head_shape, head_shape,
                   jax.ShapeDtypeStruct((b, HEAD_PAIRS, s, LANES), BF16)],
        compiler_params=_params(2),
        name="mla_projections",
    )(x, mod, pos, _rope_tables(), waq, wakv, wape, q_norm.reshape(1, -1), kv_norm.reshape(1, -1), wuq, wuk, wuv)


def _attn_kernel(q_ref, k_ref, v_ref, o_ref, m_sc, l_sc, acc_sc):
    qi = pl.program_id(2)
    nrep = TQ // LANES
    m_sc[...] = jnp.full_like(m_sc, NEG_BIG)
    l_sc[...] = jnp.zeros_like(l_sc)
    acc_sc[...] = jnp.zeros_like(acc_sc)

    def step(kb, masked):
        start = pl.multiple_of(kb * TQ, TQ)
        v = v_ref[0, 0, pl.ds(start, TQ), :]
        scores = [lax.dot_general(q_ref[0, hh], k_ref[0, hh, pl.ds(start, TQ), :], (((1,), (1,)), ((), ())),
                                  preferred_element_type=F32) for hh in range(2)]
        for hh in range(2):
            s = scores[hh]
            if masked:
                row = lax.broadcasted_iota(jnp.int32, (TQ, TQ), 0)
                col = lax.broadcasted_iota(jnp.int32, (TQ, TQ), 1)
                s = jnp.where(col <= row, s, NEG_BIG)
            m_prev = m_sc[hh]
            m_new = jnp.maximum(m_prev, jnp.max(s, axis=-1, keepdims=True))
            p = jnp.exp2(s - jnp.concatenate([m_new] * nrep, axis=1))
            alpha = jnp.exp2(m_prev - m_new)
            psum = p[:, 0:LANES]
            for t in range(1, nrep):
                psum = psum + p[:, t * LANES:(t + 1) * LANES]
            l_sc[hh] = alpha * l_sc[hh] + psum
            acc_sc[hh] = alpha * acc_sc[hh] + _dot(p.astype(BF16), v)
            m_sc[hh] = m_new

    def two_blocks(j, _):
        step(2 * j, False)
        step(2 * j + 1, False)
        return 0

    lax.fori_loop(0, lax.shift_right_logical(qi, 1), two_blocks, 0)

    @pl.when((qi & 1) == 1)
    def _():
        step(qi - 1, False)

    step(qi, True)
    for hh in range(2):
        out = acc_sc[hh] / jnp.sum(l_sc[hh], axis=-1, keepdims=True)
        half = slice(hh * V_HEAD, (hh + 1) * V_HEAD)
        o_ref[0, :, half] = out[:, half].astype(BF16)


def _attention(q, k, v):
    b, _, s, _ = q.shape
    return pl.pallas_call(
        _attn_kernel,
        grid=(b, HEAD_PAIRS, s // TQ),
        in_specs=[pl.BlockSpec((1, 2, TQ, HEAD_BLOCK), lambda i, h, j: (i, h, j, 0)),
                  pl.BlockSpec((1, 2, s, HEAD_BLOCK), lambda i, h, j: (i, h, 0, 0)),
                  pl.BlockSpec((1, 1, s, LANES), lambda i, h, j: (i, h, 0, 0))],
        out_specs=pl.BlockSpec((1, TQ, LANES), lambda i, h, j: (i, j, h)),
        out_shape=jax.ShapeDtypeStruct((b, s, MLA_HEADS * V_HEAD), BF16),
        scratch_shapes=[pltpu.VMEM((2, TQ, LANES), F32)] * 3,
        compiler_params=_params(3),
        name="mla_attention",
    )(q, k, v)


def _out_proj_kernel(x_ref, a_ref, mod_ref, w_ref, lng_ref, lnb_ref, o_ref):
    y = _dot(a_ref[0], w_ref[...])
    g = mod_ref[0, 2:3, :]
    o_ref[0] = _resid_ln(x_ref[0], y, g, lng_ref[...], lnb_ref[...])


def _out_proj_sublayer(x, attn, mod, w_o, ln_g, ln_b):
    b, s, d = x.shape
    return pl.pallas_call(
        _out_proj_kernel,
        grid=(b, s // TM),
        in_specs=[pl.BlockSpec((1, TM, d), lambda i, j: (i, j, 0)),
                  pl.BlockSpec((1, TM, d), lambda i, j: (i, j, 0)),
                  pl.BlockSpec((1, 6, d), lambda i, j: (i, 0, 0)),
                  _full(w_o.shape), _full((1, d)), _full((1, d))],
        out_specs=pl.BlockSpec((1, TM, d), lambda i, j: (i, j, 0)),
        out_shape=jax.ShapeDtypeStruct(x.shape, F32),
        compiler_params=_params(2),
        name="mla_out_proj",
    )(x, attn, mod, w_o.astype(BF16), ln_g.reshape(1, d), ln_b.reshape(1, d))


def kernel(x, c, positions, mod_w, mod_b, ln_g, ln_b, pool_w, pool_scale, mla_w_a, mla_q_norm, mla_w_uq,
           mla_kv_norm, mla_w_ukv, mla_w_o, sc_w_in, sc_conv, sc_w_out, ffn_w_up, ffn_conv, ffn_conv_b, ffn_w_down):
    b = x.shape[0]
    mods = _modulation(c, mod_w, mod_b).reshape(DEPTH, b, 6, D_MODEL)
    for i in range(DEPTH):
        mod = mods[i]
        kind, j = i % 3, i // 3
        if kind == 0:
            x = _pool_sublayer(x, mod, pool_w[j], pool_scale[j], ln_g[i, 0], ln_b[i, 0])
        elif kind == 1:
            q, k, v = _mla_projections(x, mod, positions, mla_w_a[j], mla_q_norm[j], mla_w_uq[j],
                                       mla_kv_norm[j], mla_w_ukv[j])
            attn = _attention(q, k, v)
            x = _out_proj_sublayer(x, attn, mod, mla_w_o[j], ln_g[i, 0], ln_b[i, 0])
        else:
            x = _sconv_sublayer(x, mod, sc_w_in[j], sc_conv[j], sc_w_out[j], ln_g[i, 0], ln_b[i, 0])
        x = _ffn_sublayer(x, mod, ffn_w_up[i], ffn_conv[i], ffn_conv_b[i], ffn_w_down[i], ln_g[i, 1], ln_b[i, 1])
    return x
```

```python
import functools

import numpy as np
import jax
import jax.numpy as jnp
from jax import lax
from jax.experimental import pallas as pl
from jax.experimental.pallas import tpu as pltpu

D_MODEL = 1024
DEPTH = 4
POOL_WINDOWS = (2, 4, 8, 16)
POOL_GROUP = D_MODEL // len(POOL_WINDOWS)
POOL_HALO = 16
POOL_ROWS = 128
MLA_HEADS = 16
HEAD_PAIRS = MLA_HEADS // 2
QK_NOPE = 64
QK_ROPE = 32
V_HEAD = 64
Q_LORA = 768
KV_LORA = 256
ROPE_THETA = 10000.0
FFN_HIDDEN = 2816
DEEPNORM_ALPHA = (2 * DEPTH) ** 0.25
LN_EPS = 1e-5
RMS_EPS = 1e-6
SM_SCALE = (QK_NOPE + QK_ROPE) ** -0.5
LOG2E = 1.4426950408889634
NEG_BIG = -1e30

LANES = 128
SUBLANES = 8
HEAD_BLOCK = LANES
ROPE_LO = QK_NOPE
ROPE_HALF = QK_ROPE // 2

TM = 512
ROW_BLOCK = 128
FFN_CHUNK = 256
DOWN_GROUP = 4
TQ = 512
MOD_TN = 1536
VMEM_LIMIT = 56 * 1024 * 1024

BF16 = jnp.bfloat16
F32 = jnp.float32


def _silu(v):
    return v * (1.0 / (1.0 + jnp.exp(-v)))


def _dot(a, b):
    return jnp.dot(a, b, preferred_element_type=F32)


def _resid_ln(x, y, g, ln_g, ln_b):
    r = DEEPNORM_ALPHA * x + (1.0 + g) * y
    mu = jnp.mean(r, axis=-1, keepdims=True)
    d = r - mu
    var = jnp.mean(d * d, axis=-1, keepdims=True)
    return d * lax.rsqrt(var + LN_EPS) * ln_g + ln_b


def _params(n_grid):
    return pltpu.CompilerParams(dimension_semantics=("arbitrary",) * n_grid,
                                vmem_limit_bytes=VMEM_LIMIT)


def _full(shape):
    n = len(shape)
    return pl.BlockSpec(shape, lambda *_: (0,) * n)


def _mod_kernel(c_ref, w_ref, b_ref, o_ref):
    cond = _silu(c_ref[...]).astype(BF16)
    o_ref[0] = _dot(cond, w_ref[0].astype(BF16)) + b_ref[0]


def _modulation(c, mod_w, mod_b):
    b = c.shape[0]
    n = mod_w.shape[-1]
    return pl.pallas_call(
        _mod_kernel,
        grid=(DEPTH, n // MOD_TN),
        in_specs=[_full((b, D_MODEL)),
                  pl.BlockSpec((1, D_MODEL, MOD_TN), lambda i, j: (i, 0, j)),
                  pl.BlockSpec((1, 1, MOD_TN), lambda i, j: (i, 0, j))],
        out_specs=pl.BlockSpec((1, b, MOD_TN), lambda i, j: (i, 0, j)),
        out_shape=jax.ShapeDtypeStruct((DEPTH, b, n), F32),
        compiler_params=_params(2),
        name="modulation",
    )(c, mod_w, mod_b.reshape(DEPTH, 1, n))


def _mlp_kernel(kind, row0, fc, nchunks, x_ref, mod_ref, wup_ref, cw_ref, cb_ref, wdn_ref, lng_ref, lnb_ref,
                o_ref, ubuf, pbuf, hbuf, abuf, carry, acc):
    @pl.when(pl.program_id(1) == 0)
    def _():
        carry[...] = jnp.zeros_like(carry)

    x = x_ref[0]
    sh = mod_ref[0, row0:row0 + 1, :]
    sc = mod_ref[0, row0 + 1:row0 + 2, :]
    ubuf[...] = (x * (1.0 + sc) + sh).astype(BF16)

    def up(c, slot):
        hb = hbuf.at[slot]
        pb = pbuf.at[slot]
        hb[0:SUBLANES, :] = carry[c]
        parts = 2 if kind == "ffn" else 3
        n = wup_ref.shape[1] // parts
        for p in range(parts):
            res = _dot(ubuf[...], wup_ref[:, p * n + c * fc:p * n + (c + 1) * fc])
            if kind == "ffn":
                hb[SUBLANES:SUBLANES + TM, p * fc:(p + 1) * fc] = res
            else:
                pb[:, p * fc:(p + 1) * fc] = res
        if kind == "sconv":
            for r in range(0, TM, ROW_BLOCK):
                hb[SUBLANES + r:SUBLANES + r + ROW_BLOCK, :] = (
                    pb[r:r + ROW_BLOCK, fc:2 * fc] * pb[r:r + ROW_BLOCK, 2 * fc:3 * fc])
        carry[c] = hb[TM:TM + SUBLANES, :]

    def activate(c, slot):
        hb = hbuf.at[slot]
        pb = pbuf.at[slot]
        cw = cw_ref[c]
        cb = cb_ref[c]

        def conv3(r, cols):
            ext = hb[r:r + SUBLANES + ROW_BLOCK, cols]
            return (cw[2:3, cols] * ext[SUBLANES:, :]
                    + cw[1:2, cols] * pltpu.roll(ext, 1, 0)[SUBLANES:, :]
                    + cw[0:1, cols] * pltpu.roll(ext, 2, 0)[SUBLANES:, :])

        for lo in range(0, fc, LANES):
            val = slice(lo, lo + LANES)
            gate = slice(fc + lo, fc + lo + LANES)
            for r in range(0, TM, ROW_BLOCK):
                if kind == "ffn":
                    a = _silu(conv3(r, gate) + cb[:, gate]) * (conv3(r, val) + cb[:, val])
                else:
                    a = pb[r:r + ROW_BLOCK, val] * conv3(r, val)
                abuf[r:r + ROW_BLOCK, c * fc + lo:c * fc + lo + LANES] = a.astype(BF16)

    def down(c0, c1):
        return _dot(abuf[:, c0 * fc:c1 * fc], wdn_ref[c0 * fc:c1 * fc, :])

    y = None
    up(0, 0)
    for c in range(nchunks):
        if c + 1 < nchunks:
            up(c + 1, (c + 1) % 2)
        activate(c, c % 2)
        c0 = c - c % DOWN_GROUP
        if c + 1 == nchunks:
            y = down(c0, c + 1) if c0 == 0 else acc[...] + down(c0, c + 1)
        elif (c + 1) % DOWN_GROUP == 0:
            if c0 == 0:
                acc[...] = down(c0, c + 1)
            else:
                acc[...] += down(c0, c + 1)

    g = mod_ref[0, row0 + 2:row0 + 3, :]
    o_ref[0] = _resid_ln(x, y, g, lng_ref[...], lnb_ref[...])


def _mlp_sublayer(kind, x, mod, row0, wup, cw, cb, wdn, ln_g, ln_b):
    b, s, d = x.shape
    nchunks = cw.shape[0]
    fc = wdn.shape[0] // nchunks
    w = wup.shape[1] // nchunks
    wc = cw.shape[-1]
    kern = functools.partial(_mlp_kernel, kind, row0, fc, nchunks)
    pshape = (TM, w) if kind == "sconv" else (SUBLANES, LANES)
    return pl.pallas_call(
        kern,
        grid=(b, s // TM),
        in_specs=[pl.BlockSpec((1, TM, d), lambda i, j: (i, j, 0)),
                  pl.BlockSpec((1, 6, d), lambda i, j: (i, 0, 0)),
                  _full(wup.shape), _full(cw.shape), _full(cb.shape), _full(wdn.shape),
                  _full((1, d)), _full((1, d))],
        out_specs=pl.BlockSpec((1, TM, d), lambda i, j: (i, j, 0)),
        out_shape=jax.ShapeDtypeStruct(x.shape, F32),
        scratch_shapes=[pltpu.VMEM((TM, d), BF16),
                        pltpu.VMEM((2,) + pshape, F32),
                        pltpu.VMEM((2, TM + SUBLANES, wc), F32),
                        pltpu.VMEM((TM, nchunks * fc), BF16),
                        pltpu.VMEM((nchunks, SUBLANES, wc), F32),
                        pltpu.VMEM((TM, d), F32)],
        compiler_params=_params(2),
        name=kind + "_sublayer",
    )(x, mod, wup, cw, cb, wdn, ln_g.reshape(1, d), ln_b.reshape(1, d))


def _chunk_cols(w, parts, fc):
    lead = w.shape[:-1]
    n = w.shape[-1] // parts
    w = w.reshape(lead + (parts, n // fc, fc))
    w = jnp.moveaxis(w, -2, 0)
    return w.reshape((n // fc,) + lead + (parts * fc,))


def _ffn_sublayer(x, mod, w_up, conv_w, conv_b, w_down, ln_g, ln_b):
    fc = FFN_CHUNK
    wup = w_up.astype(BF16)
    cw = _chunk_cols(conv_w, 2, fc)
    cb = _chunk_cols(conv_b.reshape(1, -1), 2, fc)
    wdn = w_down.astype(BF16)
    return _mlp_sublayer("ffn", x, mod, 3, wup, cw, cb, wdn, ln_g, ln_b)


def _sconv_sublayer(x, mod, w_in, conv_w, w_out, ln_g, ln_b):
    fc = FFN_CHUNK
    wup = w_in.astype(BF16)
    cw = _chunk_cols(conv_w, 1, fc)
    cb = jnp.zeros((D_MODEL // fc, 1, fc), F32)
    wdn = w_out.astype(BF16)
    return _mlp_sublayer("sconv", x, mod, 0, wup, cw, cb, wdn, ln_g, ln_b)


def _pool_kernel(x_ref, mod_ref, w_ref, scale_ref, lng_ref, lnb_ref, o_ref, ubuf, pbuf, ybuf):
    j = pl.program_id(1)

    @pl.when(j == 0)
    def _():
        ubuf[0:POOL_HALO, :] = jnp.zeros((POOL_HALO, D_MODEL), F32)

    @pl.when(j > 0)
    def _():
        ubuf[0:POOL_HALO, :] = ubuf[TM:TM + POOL_HALO, :]

    x = x_ref[0]
    sh = mod_ref[0, 0:1, :]
    sc = mod_ref[0, 1:2, :]
    ubuf[POOL_HALO:POOL_HALO + TM, :] = x * (1.0 + sc) + sh
    for gi, win in enumerate(POOL_WINDOWS):
        lo = gi * POOL_GROUP
        cols = slice(lo, lo + POOL_GROUP)
        for r in range(0, TM, POOL_ROWS):
            ext = ubuf[r:r + POOL_HALO + POOL_ROWS, cols]
            tot = ext
            shift = 1
            while shift < win:
                tot = tot + pltpu.roll(tot, shift, 0)
                shift *= 2
            t = j * TM + r + lax.broadcasted_iota(jnp.int32, (POOL_ROWS, 1), 0)
            count = jnp.minimum(t + 1, win).astype(F32)
            pooled = tot[POOL_HALO:, :] / count - ext[POOL_HALO:, :]
            pbuf[r:r + POOL_ROWS, cols] = pooled.astype(BF16)
        ybuf[:, cols] = _dot(pbuf[:, cols], w_ref[gi])
    y = ybuf[...] * scale_ref[...]
    g = mod_ref[0, 2:3, :]
    o_ref[0] = _resid_ln(x, y, g, lng_ref[...], lnb_ref[...])


def _pool_sublayer(x, mod, w_groups, scale, ln_g, ln_b):
    b, s, d = x.shape
    return pl.pallas_call(
        _pool_kernel,
        grid=(b, s // TM),
        in_specs=[pl.BlockSpec((1, TM, d), lambda i, j: (i, j, 0)),
                  pl.BlockSpec((1, 6, d), lambda i, j: (i, 0, 0)),
                  _full(w_groups.shape), _full((1, d)), _full((1, d)), _full((1, d))],
        out_specs=pl.BlockSpec((1, TM, d), lambda i, j: (i, j, 0)),
        out_shape=jax.ShapeDtypeStruct(x.shape, F32),
        scratch_shapes=[pltpu.VMEM((TM + POOL_HALO, d), F32),
                        pltpu.VMEM((TM, d), BF16),
                        pltpu.VMEM((TM, d), F32)],
        compiler_params=_params(2),
        name="pool_sublayer",
    )(x, mod, w_groups.astype(BF16), scale.reshape(1, d), ln_g.reshape(1, d), ln_b.reshape(1, d))


def _rope_tables():
    inv_freq = ROPE_THETA ** (-np.arange(0, QK_ROPE, 2, dtype=np.float32) / QK_ROPE)
    tab = np.zeros((SUBLANES, LANES), np.float32)
    a, m, e = ROPE_LO, ROPE_LO + ROPE_HALF, ROPE_LO + QK_ROPE
    tab[0, a:m] = inv_freq
    tab[0, m:e] = inv_freq
    tab[1, :a] = 1.0
    tab[2, a:e] = 1.0
    tab[3, a:m] = -1.0
    tab[4, m:e] = 1.0
    return jnp.asarray(tab)


def _rms(v, gain):
    return v * lax.rsqrt(jnp.mean(v * v, axis=-1, keepdims=True) + RMS_EPS) * gain


def _mla_proj_kernel(x_ref, mod_ref, pos_ref, tab_ref, waq_ref, wakv_ref, wape_ref, qn_ref, kvn_ref,
                     wuq_ref, wuk_ref, wuv_ref, q_ref, k_ref, v_ref):
    x = x_ref[0]
    sh = mod_ref[0, 0:1, :]
    sc = mod_ref[0, 1:2, :]
    u = (x * (1.0 + sc) + sh).astype(BF16)
    cq = _rms(_dot(u, waq_ref[...]), qn_ref[...]).astype(BF16)
    ckv = _rms(_dot(u, wakv_ref[...]), kvn_ref[...]).astype(BF16)
    ape = _dot(u, wape_ref[...])

    tab = tab_ref[...]
    ang = pos_ref[0] * tab[0:1, :]
    cs = jnp.cos(ang)
    sn = jnp.sin(ang)
    keep = tab[1:2, :] + tab[2:3, :] * cs
    s_lo = tab[3:4, :] * sn
    s_hi = tab[4:5, :] * sn

    def rope(v):
        return (v * keep + pltpu.roll(v, LANES - ROPE_HALF, 1) * s_lo
                + pltpu.roll(v, ROPE_HALF, 1) * s_hi)

    kpe = rope(ape)
    for hp in range(HEAD_PAIRS):
        q2 = _dot(cq, wuq_ref[hp])
        k2 = _dot(ckv, wuk_ref[hp])
        for hh in range(2):
            cols = slice(hh * HEAD_BLOCK, (hh + 1) * HEAD_BLOCK)
            q_ref[0, 2 * hp + hh] = (rope(q2[:, cols]) * (SM_SCALE * LOG2E)).astype(BF16)
            k_ref[0, 2 * hp + hh] = (k2[:, cols] + kpe).astype(BF16)
    v = _dot(ckv, wuv_ref[...])
    for hp in range(HEAD_PAIRS):
        v_ref[0, hp] = v[:, hp * LANES:(hp + 1) * LANES].astype(BF16)


def _mla_projections(x, mod, positions, w_a, q_norm, w_uq, kv_norm, w_ukv):
    b, s, d = x.shape
    w_a = w_a.astype(BF16)
    waq = w_a[:, :Q_LORA]
    wakv = w_a[:, Q_LORA:Q_LORA + KV_LORA]
    wape = jnp.zeros((d, HEAD_BLOCK), BF16).at[:, ROPE_LO:ROPE_LO + QK_ROPE].set(w_a[:, Q_LORA + KV_LORA:])
    wuq = w_uq.astype(BF16).reshape(Q_LORA, MLA_HEADS, QK_NOPE + QK_ROPE)
    wuq = jnp.pad(wuq, ((0, 0), (0, 0), (0, HEAD_BLOCK - QK_NOPE - QK_ROPE)))
    wuq = wuq.reshape(Q_LORA, HEAD_PAIRS, 2 * HEAD_BLOCK).transpose(1, 0, 2)
    wukv = w_ukv.astype(BF16).reshape(KV_LORA, MLA_HEADS, QK_NOPE + V_HEAD)
    wuk = jnp.pad(wukv[:, :, :QK_NOPE], ((0, 0), (0, 0), (0, HEAD_BLOCK - QK_NOPE)))
    wuk = wuk.reshape(KV_LORA, HEAD_PAIRS, 2 * HEAD_BLOCK).transpose(1, 0, 2)
    wuv = wukv[:, :, QK_NOPE:].reshape(KV_LORA, MLA_HEADS * V_HEAD)
    pos = positions.astype(F32).reshape(b, s, 1)
    head_shape = jax.ShapeDtypeStruct((b, MLA_HEADS, s, HEAD_BLOCK), BF16)
    return pl.pallas_call(
        _mla_proj_kernel,
        grid=(b, s // TM),
        in_specs=[pl.BlockSpec((1, TM, d), lambda i, j: (i, j, 0)),
                  pl.BlockSpec((1, 6, d), lambda i, j: (i, 0, 0)),
                  pl.BlockSpec((1, TM, 1), lambda i, j: (i, j, 0)),
                  _full((SUBLANES, LANES)),
                  _full(waq.shape), _full(wakv.shape), _full(wape.shape),
                  _full((1, Q_LORA)), _full((1, KV_LORA)),
                  _full(wuq.shape), _full(wuk.shape), _full(wuv.shape)],
        out_specs=[pl.BlockSpec((1, MLA_HEADS, TM, HEAD_BLOCK), lambda i, j: (i, 0, j, 0)),
                   pl.BlockSpec((1, MLA_HEADS, TM, HEAD_BLOCK), lambda i, j: (i, 0, j, 0)),
                   pl.BlockSpec((1, HEAD_PAIRS, TM, LANES), lambda i, j: (i, 0, j, 0))],
        out_shape=[head_shape, head_shape,
                   jax.ShapeDtypeStruct((b, HEAD_PAIRS, s, LANES), BF16)],
        compiler_params=_params(2),
        name="mla_projections",
    )(x, mod, pos, _rope_tables(), waq, wakv, wape, q_norm.reshape(1, -1), kv_norm.reshape(1, -1), wuq, wuk, wuv)


def _attn_kernel(q_ref, k_ref, v_ref, o_ref, m_sc, l_sc, acc_sc):
    qi = pl.program_id(2)
    nrep = TQ // LANES
    m_sc[...] = jnp.full_like(m_sc, NEG_BIG)
    l_sc[...] = jnp.zeros_like(l_sc)
    acc_sc[...] = jnp.zeros_like(acc_sc)

    def step(kb, masked):
        start = pl.multiple_of(kb * TQ, TQ)
        v = v_ref[0, 0, pl.ds(start, TQ), :]
        scores = [lax.dot_general(q_ref[0, hh], k_ref[0, hh, pl.ds(start, TQ), :], (((1,), (1,)), ((), ())),
                                  preferred_element_type=F32) for hh in range(2)]
        for hh in range(2):
            s = scores[hh]
            if masked:
                row = lax.broadcasted_iota(jnp.int32, (TQ, TQ), 0)
                col = lax.broadcasted_iota(jnp.int32, (TQ, TQ), 1)
                s = jnp.where(col <= row, s, NEG_BIG)
            m_prev = m_sc[hh]
            m_new = jnp.maximum(m_prev, jnp.max(s, axis=-1, keepdims=True))
            p = jnp.exp2(s - jnp.concatenate([m_new] * nrep, axis=1))
            alpha = jnp.exp2(m_prev - m_new)
            psum = p[:, 0:LANES]
            for t in range(1, nrep):
                psum = psum + p[:, t * LANES:(t + 1) * LANES]
            l_sc[hh] = alpha * l_sc[hh] + psum
            acc_sc[hh] = alpha * acc_sc[hh] + _dot(p.astype(BF16), v)
            m_sc[hh] = m_new

    def two_blocks(j, _):
        step(2 * j, False)
        step(2 * j + 1, False)
        return 0

    lax.fori_loop(0, lax.shift_right_logical(qi, 1), two_blocks, 0)

    @pl.when((qi & 1) == 1)
    def _():
        step(qi - 1, False)

    step(qi, True)
    for hh in range(2):
        out = acc_sc[hh] / jnp.sum(l_sc[hh], axis=-1, keepdims=True)
        half = slice(hh * V_HEAD, (hh + 1) * V_HEAD)
        o_ref[0, :, half] = out[:, half].astype(BF16)


def _attention(q, k, v):
    b, _, s, _ = q.shape
    return pl.pallas_call(
        _attn_kernel,
        grid=(b, HEAD_PAIRS, s // TQ),
        in_specs=[pl.BlockSpec((1, 2, TQ, HEAD_BLOCK), lambda i, h, j: (i, h, j, 0)),
                  pl.BlockSpec((1, 2, s, HEAD_BLOCK), lambda i, h, j: (i, h, 0, 0)),
                  pl.BlockSpec((1, 1, s, LANES), lambda i, h, j: (i, h, 0, 0))],
        out_specs=pl.BlockSpec((1, TQ, LANES), lambda i, h, j: (i, j, h)),
        out_shape=jax.ShapeDtypeStruct((b, s, MLA_HEADS * V_HEAD), BF16),
        scratch_shapes=[pltpu.VMEM((2, TQ, LANES), F32)] * 3,
        compiler_params=_params(3),
        name="mla_attention",
    )(q, k, v)


def _out_proj_kernel(x_ref, a_ref, mod_ref, w_ref, lng_ref, lnb_ref, o_ref):
    y = _dot(a_ref[0], w_ref[...])
    g = mod_ref[0, 2:3, :]
    o_ref[0] = _resid_ln(x_ref[0], y, g, lng_ref[...], lnb_ref[...])


def _out_proj_sublayer(x, attn, mod, w_o, ln_g, ln_b):
    b, s, d = x.shape
    return pl.pallas_call(
        _out_proj_kernel,
        grid=(b, s // TM),
        in_specs=[pl.BlockSpec((1, TM, d), lambda i, j: (i, j, 0)),
                  pl.BlockSpec((1, TM, d), lambda i, j: (i, j, 0)),
                  pl.BlockSpec((1, 6, d), lambda i, j: (i, 0, 0)),
                  _full(w_o.shape), _full((1, d)), _full((1, d))],
        out_specs=pl.BlockSpec((1, TM, d), lambda i, j: (i, j, 0)),
        out_shape=jax.ShapeDtypeStruct(x.shape, F32),
        compiler_params=_params(2),
        name="mla_out_proj",
    )(x, attn, mod, w_o.astype(BF16), ln_g.reshape(1, d), ln_b.reshape(1, d))


def kernel(x, c, positions, mod_w, mod_b, ln_g, ln_b, pool_w, pool_scale, mla_w_a, mla_q_norm, mla_w_uq,
           mla_kv_norm, mla_w_ukv, mla_w_o, sc_w_in, sc_conv, sc_w_out, ffn_w_up, ffn_conv, ffn_conv_b, ffn_w_down):
    b = x.shape[0]
    mods = _modulation(c, mod_w, mod_b).reshape(DEPTH, b, 6, D_MODEL)
    for i in range(DEPTH):
        mod = mods[i]
        kind, j = i % 3, i // 3
        if kind == 0:
            x = _pool_sublayer(x, mod, pool_w[j], pool_scale[j], ln_g[i, 0], ln_b[i, 0])
        elif kind == 1:
            q, k, v = _mla_projections(x, mod, positions, mla_w_a[j], mla_q_norm[j], mla_w_uq[j],
                                       mla_kv_norm[j], mla_w_ukv[j])
            attn = _attention(q, k, v)
            x = _out_proj_sublayer(x, attn, mod, mla_w_o[j], ln_g[i, 0], ln_b[i, 0])
        else:
            x = _sconv_sublayer(x, mod, sc_w_in[j], sc_conv[j], sc_w_out[j], ln_g[i, 0], ln_b[i, 0])
        x = _ffn_sublayer(x, mod, ffn_w_up[i], ffn_conv[i], ffn_conv_b[i], ffn_w_down[i], ln_g[i, 1], ln_b[i, 1])
    return x
```

```python
import functools

import numpy as np
import jax
import jax.numpy as jnp
from jax import lax
from jax.experimental import pallas as pl
from jax.experimental.pallas import tpu as pltpu

D_MODEL = 1024
DEPTH = 4
POOL_WINDOWS = (2, 4, 8, 16)
POOL_GROUP = D_MODEL // len(POOL_WINDOWS)
POOL_HALO = 16
POOL_ROWS = 128
MLA_HEADS = 16
HEAD_PAIRS = MLA_HEADS // 2
QK_NOPE = 64
QK_ROPE = 32
V_HEAD = 64
Q_LORA = 768
KV_LORA = 256
ROPE_THETA = 10000.0
FFN_HIDDEN = 2816
DEEPNORM_ALPHA = (2 * DEPTH) ** 0.25
LN_EPS = 1e-5
RMS_EPS = 1e-6
SM_SCALE = (QK_NOPE + QK_ROPE) ** -0.5
LOG2E = 1.4426950408889634
NEG_BIG = -1e30

LANES = 128
SUBLANES = 8
HEAD_BLOCK = LANES
ROPE_LO = QK_NOPE
ROPE_HALF = QK_ROPE // 2

TM = 512
ROW_BLOCK = 64
FFN_CHUNK = 256
DOWN_GROUP = 4
UP_AHEAD = 2
TQ = 512
MOD_TN = 1536
VMEM_LIMIT = 56 * 1024 * 1024

BF16 = jnp.bfloat16
F32 = jnp.float32


def _silu(v):
    return v * (1.0 / (1.0 + jnp.exp(-v)))


def _dot(a, b):
    return jnp.dot(a, b, preferred_element_type=F32)


def _resid_ln(x, y, g, ln_g, ln_b):
    r = DEEPNORM_ALPHA * x + (1.0 + g) * y
    mu = jnp.mean(r, axis=-1, keepdims=True)
    d = r - mu
    var = jnp.mean(d * d, axis=-1, keepdims=True)
    return d * lax.rsqrt(var + LN_EPS) * ln_g + ln_b


def _params(n_grid):
    return pltpu.CompilerParams(dimension_semantics=("arbitrary",) * n_grid,
                                vmem_limit_bytes=VMEM_LIMIT)


def _full(shape):
    n = len(shape)
    return pl.BlockSpec(shape, lambda *_: (0,) * n)


def _mod_kernel(c_ref, w_ref, b_ref, o_ref):
    cond = _silu(c_ref[...]).astype(BF16)
    o_ref[0] = _dot(cond, w_ref[0].astype(BF16)) + b_ref[0]


def _modulation(c, mod_w, mod_b):
    b = c.shape[0]
    n = mod_w.shape[-1]
    return pl.pallas_call(
        _mod_kernel,
        grid=(DEPTH, n // MOD_TN),
        in_specs=[_full((b, D_MODEL)),
                  pl.BlockSpec((1, D_MODEL, MOD_TN), lambda i, j: (i, 0, j)),
                  pl.BlockSpec((1, 1, MOD_TN), lambda i, j: (i, 0, j))],
        out_specs=pl.BlockSpec((1, b, MOD_TN), lambda i, j: (i, 0, j)),
        out_shape=jax.ShapeDtypeStruct((DEPTH, b, n), F32),
        compiler_params=_params(2),
        name="modulation",
    )(c, mod_w, mod_b.reshape(DEPTH, 1, n))


def _mlp_kernel(kind, row0, fc, nchunks, x_ref, mod_ref, wup_ref, cw_ref, cb_ref, wdn_ref, lng_ref, lnb_ref,
                o_ref, ubuf, pbuf, hbuf, abuf, carry, acc):
    @pl.when(pl.program_id(1) == 0)
    def _():
        carry[...] = jnp.zeros_like(carry)

    x = x_ref[0]
    sh = mod_ref[0, row0:row0 + 1, :]
    sc = mod_ref[0, row0 + 1:row0 + 2, :]
    ubuf[...] = (x * (1.0 + sc) + sh).astype(BF16)

    def up(c, slot):
        hb = hbuf.at[slot]
        pb = pbuf.at[slot]
        hb[0:SUBLANES, :] = carry[c]
        parts = 2 if kind == "ffn" else 3
        n = wup_ref.shape[1] // parts
        for p in range(parts):
            res = _dot(ubuf[...], wup_ref[:, p * n + c * fc:p * n + (c + 1) * fc])
            if kind == "ffn":
                hb[SUBLANES:SUBLANES + TM, p * fc:(p + 1) * fc] = res
            else:
                pb[:, p * fc:(p + 1) * fc] = res
        if kind == "sconv":
            for r in range(0, TM, ROW_BLOCK):
                hb[SUBLANES + r:SUBLANES + r + ROW_BLOCK, :] = (
                    pb[r:r + ROW_BLOCK, fc:2 * fc] * pb[r:r + ROW_BLOCK, 2 * fc:3 * fc])
        carry[c] = hb[TM:TM + SUBLANES, :]

    def activate(c, slot):
        hb = hbuf.at[slot]
        pb = pbuf.at[slot]
        cw = cw_ref[c]
        cb = cb_ref[c]

        def conv3(r, cols):
            return (cw[2:3, cols] * hb[SUBLANES + r:SUBLANES + r + ROW_BLOCK, cols]
                    + cw[1:2, cols] * hb[SUBLANES - 1 + r:SUBLANES - 1 + r + ROW_BLOCK, cols]
                    + cw[0:1, cols] * hb[SUBLANES - 2 + r:SUBLANES - 2 + r + ROW_BLOCK, cols])

        for lo in range(0, fc, LANES):
            val = slice(lo, lo + LANES)
            gate = slice(fc + lo, fc + lo + LANES)
            for r in range(0, TM, ROW_BLOCK):
                if kind == "ffn":
                    a = _silu(conv3(r, gate) + cb[:, gate]) * (conv3(r, val) + cb[:, val])
                else:
                    a = pb[r:r + ROW_BLOCK, val] * conv3(r, val)
                abuf[r:r + ROW_BLOCK, c * fc + lo:c * fc + lo + LANES] = a.astype(BF16)

    def down(c0, c1):
        return _dot(abuf[:, c0 * fc:c1 * fc], wdn_ref[c0 * fc:c1 * fc, :])

    y = None
    for c in range(min(UP_AHEAD, nchunks)):
        up(c, c % (UP_AHEAD + 1))
    for c in range(nchunks):
        if c + UP_AHEAD < nchunks:
            up(c + UP_AHEAD, (c + UP_AHEAD) % (UP_AHEAD + 1))
        activate(c, c % (UP_AHEAD + 1))
        c0 = c - c % DOWN_GROUP
        if c + 1 == nchunks:
            y = down(c0, c + 1) if c0 == 0 else acc[...] + down(c0, c + 1)
        elif (c + 1) % DOWN_GROUP == 0:
            if c0 == 0:
                acc[...] = down(c0, c + 1)
            else:
                acc[...] += down(c0, c + 1)

    g = mod_ref[0, row0 + 2:row0 + 3, :]
    o_ref[0] = _resid_ln(x, y, g, lng_ref[...], lnb_ref[...])


def _mlp_sublayer(kind, x, mod, row0, wup, cw, cb, wdn, ln_g, ln_b):
    b, s, d = x.shape
    nchunks = cw.shape[0]
    fc = wdn.shape[0] // nchunks
    w = wup.shape[1] // nchunks
    wc = cw.shape[-1]
    kern = functools.partial(_mlp_kernel, kind, row0, fc, nchunks)
    pshape = (TM, w) if kind == "sconv" else (SUBLANES, LANES)
    return pl.pallas_call(
        kern,
        grid=(b, s // TM),
        in_specs=[pl.BlockSpec((1, TM, d), lambda i, j: (i, j, 0)),
                  pl.BlockSpec((1, 6, d), lambda i, j: (i, 0, 0)),
                  _full(wup.shape), _full(cw.shape), _full(cb.shape), _full(wdn.shape),
                  _full((1, d)), _full((1, d))],
        out_specs=pl.BlockSpec((1, TM, d), lambda i, j: (i, j, 0)),
        out_shape=jax.ShapeDtypeStruct(x.shape, F32),
        scratch_shapes=[pltpu.VMEM((TM, d), BF16),
                        pltpu.VMEM((UP_AHEAD + 1,) + pshape, F32),
                        pltpu.VMEM((UP_AHEAD + 1, TM + SUBLANES, wc), F32),
                        pltpu.VMEM((TM, nchunks * fc), BF16),
                        pltpu.VMEM((nchunks, SUBLANES, wc), F32),
                        pltpu.VMEM((TM, d), F32)],
        compiler_params=_params(2),
        name=kind + "_sublayer",
    )(x, mod, wup, cw, cb, wdn, ln_g.reshape(1, d), ln_b.reshape(1, d))


def _chunk_cols(w, parts, fc):
    lead = w.shape[:-1]
    n = w.shape[-1] // parts
    w = w.reshape(lead + (parts, n // fc, fc))
    w = jnp.moveaxis(w, -2, 0)
    return w.reshape((n // fc,) + lead + (parts * fc,))


def _ffn_sublayer(x, mod, w_up, conv_w, conv_b, w_down, ln_g, ln_b):
    fc = FFN_CHUNK
    wup = w_up.astype(BF16)
    cw = _chunk_cols(conv_w, 2, fc)
    cb = _chunk_cols(conv_b.reshape(1, -1), 2, fc)
    wdn = w_down.astype(BF16)
    return _mlp_sublayer("ffn", x, mod, 3, wup, cw, cb, wdn, ln_g, ln_b)


def _sconv_sublayer(x, mod, w_in, conv_w, w_out, ln_g, ln_b):
    fc = FFN_CHUNK
    wup = w_in.astype(BF16)
    cw = _chunk_cols(conv_w, 1, fc)
    cb = jnp.zeros((D_MODEL // fc, 1, fc), F32)
    wdn = w_out.astype(BF16)
    return _mlp_sublayer("sconv", x, mod, 0, wup, cw, cb, wdn, ln_g, ln_b)


def _pool_kernel(x_ref, mod_ref, w_ref, scale_ref, lng_ref, lnb_ref, o_ref, ubuf, pbuf, ybuf):
    j = pl.program_id(1)

    @pl.when(j == 0)
    def _():
        ubuf[0:POOL_HALO, :] = jnp.zeros((POOL_HALO, D_MODEL), F32)

    @pl.when(j > 0)
    def _():
        ubuf[0:POOL_HALO, :] = ubuf[TM:TM + POOL_HALO, :]

    x = x_ref[0]
    sh = mod_ref[0, 0:1, :]
    sc = mod_ref[0, 1:2, :]
    ubuf[POOL_HALO:POOL_HALO + TM, :] = x * (1.0 + sc) + sh
    for gi, win in enumerate(POOL_WINDOWS):
        lo = gi * POOL_GROUP
        cols = slice(lo, lo + POOL_GROUP)
        for r in range(0, TM, POOL_ROWS):
            ext = ubuf[r:r + POOL_HALO + POOL_ROWS, cols]
            tot = ext
            shift = 1
            while shift < win:
                tot = tot + pltpu.roll(tot, shift, 0)
                shift *= 2
            t = j * TM + r + lax.broadcasted_iota(jnp.int32, (POOL_ROWS, 1), 0)
            count = jnp.minimum(t + 1, win).astype(F32)
            pooled = tot[POOL_HALO:, :] / count - ext[POOL_HALO:, :]
            pbuf[r:r + POOL_ROWS, cols] = pooled.astype(BF16)
        ybuf[:, cols] = _dot(pbuf[:, cols], w_ref[gi])
    y = ybuf[...] * scale_ref[...]
    g = mod_ref[0, 2:3, :]
    o_ref[0] = _resid_ln(x, y, g, lng_ref[...], lnb_ref[...])


def _pool_sublayer(x, mod, w_groups, scale, ln_g, ln_b):
    b, s, d = x.shape
    return pl.pallas_call(
        _pool_kernel,
        grid=(b, s // TM),
        in_specs=[pl.BlockSpec((1, TM, d), lambda i, j: (i, j, 0)),
                  pl.BlockSpec((1, 6, d), lambda i, j: (i, 0, 0)),
                  _full(w_groups.shape), _full((1, d)), _full((1, d)), _full((1, d))],
        out_specs=pl.BlockSpec((1, TM, d), lambda i, j: (i, j, 0)),
        out_shape=jax.ShapeDtypeStruct(x.shape, F32),
        scratch_shapes=[pltpu.VMEM((TM + POOL_HALO, d), F32),
                        pltpu.VMEM((TM, d), BF16),
                        pltpu.VMEM((TM, d), F32)],
        compiler_params=_params(2),
        name="pool_sublayer",
    )(x, mod, w_groups.astype(BF16), scale.reshape(1, d), ln_g.reshape(1, d), ln_b.reshape(1, d))


def _rope_tables():
    inv_freq = ROPE_THETA ** (-np.arange(0, QK_ROPE, 2, dtype=np.float32) / QK_ROPE)
    tab = np.zeros((SUBLANES, LANES), np.float32)
    a, m, e = ROPE_LO, ROPE_LO + ROPE_HALF, ROPE_LO + QK_ROPE
    tab[0, a:m] = inv_freq
    tab[0, m:e] = inv_freq
    tab[1, :a] = 1.0
    tab[2, a:e] = 1.0
    tab[3, a:m] = -1.0
    tab[4, m:e] = 1.0
    return jnp.asarray(tab)


def _rms(v, gain):
    return v * lax.rsqrt(jnp.mean(v * v, axis=-1, keepdims=True) + RMS_EPS) * gain


def _mla_proj_kernel(x_ref, mod_ref, pos_ref, tab_ref, waq_ref, wakv_ref, wape_ref, qn_ref, kvn_ref,
                     wuq_ref, wuk_ref, wuv_ref, q_ref, k_ref, v_ref):
    x = x_ref[0]
    sh = mod_ref[0, 0:1, :]
    sc = mod_ref[0, 1:2, :]
    u = (x * (1.0 + sc) + sh).astype(BF16)
    cq = _rms(_dot(u, waq_ref[...]), qn_ref[...]).astype(BF16)
    ckv = _rms(_dot(u, wakv_ref[...]), kvn_ref[...]).astype(BF16)
    ape = _dot(u, wape_ref[...])

    tab = tab_ref[...]
    ang = pos_ref[0] * tab[0:1, :]
    cs = jnp.cos(ang)
    sn = jnp.sin(ang)
    keep = tab[1:2, :] + tab[2:3, :] * cs
    s_lo = tab[3:4, :] * sn
    s_hi = tab[4:5, :] * sn

    def rope(v):
        return (v * keep + pltpu.roll(v, LANES - ROPE_HALF, 1) * s_lo
                + pltpu.roll(v, ROPE_HALF, 1) * s_hi)

    kpe = rope(ape)
    for hp in range(HEAD_PAIRS):
        q2 = _dot(cq, wuq_ref[hp])
        k2 = _dot(ckv, wuk_ref[hp])
        for hh in range(2):
            cols = slice(hh * HEAD_BLOCK, (hh + 1) * HEAD_BLOCK)
            q_ref[0, 2 * hp + hh] = (rope(q2[:, cols]) * (SM_SCALE * LOG2E)).astype(BF16)
            k_ref[0, 2 * hp + hh] = (k2[:, cols] + kpe).astype(BF16)
    v = _dot(ckv, wuv_ref[...])
    for hp in range(HEAD_PAIRS):
        v_ref[0, hp] = v[:, hp * LANES:(hp + 1) * LANES].astype(BF16)


def _mla_projections(x, mod, positions, w_a, q_norm, w_uq, kv_norm, w_ukv):
    b, s, d = x.shape
    w_a = w_a.astype(BF16)
    waq = w_a[:, :Q_LORA]
    wakv = w_a[:, Q_LORA:Q_LORA + KV_LORA]
    wape = jnp.zeros((d, HEAD_BLOCK), BF16).at[:, ROPE_LO:ROPE_LO + QK_ROPE].set(w_a[:, Q_LORA + KV_LORA:])
    wuq = w_uq.astype(BF16).reshape(Q_LORA, MLA_HEADS, QK_NOPE + QK_ROPE)
    wuq = jnp.pad(wuq, ((0, 0), (0, 0), (0, HEAD_BLOCK - QK_NOPE - QK_ROPE)))
    wuq = wuq.reshape(Q_LORA, HEAD_PAIRS, 2 * HEAD_BLOCK).transpose(1, 0, 2)
    wukv = w_ukv.astype(BF16).reshape(KV_LORA, MLA_HEADS, QK_NOPE + V_HEAD)
    wuk = jnp.pad(wukv[:, :, :QK_NOPE], ((0, 0), (0, 0), (0, HEAD_BLOCK - QK_NOPE)))
    wuk = wuk.reshape(KV_LORA, HEAD_PAIRS, 2 * HEAD_BLOCK).transpose(1, 0, 2)
    wuv = wukv[:, :, QK_NOPE:].reshape(KV_LORA, MLA_HEADS * V_HEAD)
    pos = positions.astype(F32).reshape(b, s, 1)
    head_shape = jax.ShapeDtypeStruct((b, MLA_HEADS, s, HEAD_BLOCK), BF16)
    return pl.pallas_call(
        _mla_proj_kernel,
        grid=(b, s // TM),
        in_specs=[pl.BlockSpec((1, TM, d), lambda i, j: (i, j, 0)),
                  pl.BlockSpec((1, 6, d), lambda i, j: (i, 0, 0)),
                  pl.BlockSpec((1, TM, 1), lambda i, j: (i, j, 0)),
                  _full((SUBLANES, LANES)),
                  _full(waq.shape), _full(wakv.shape), _full(wape.shape),
                  _full((1, Q_LORA)), _full((1, KV_LORA)),
                  _full(wuq.shape), _full(wuk.shape), _full(wuv.shape)],
        out_specs=[pl.BlockSpec((1, MLA_HEADS, TM, HEAD_BLOCK), lambda i, j: (i, 0, j, 0)),
                   pl.BlockSpec((1, MLA_HEADS, TM, HEAD_BLOCK), lambda i, j: (i, 0, j, 0)),
                   pl.BlockSpec((1, HEAD_PAIRS, TM, LANES), lambda i, j: (i, 0, j, 0))],
        out_shape=[head_shape, head_shape,
                   jax.ShapeDtypeStruct((b, HEAD_PAIRS, s, LANES), BF16)],
        compiler_params=_params(2),
        name="mla_projections",
    )(x, mod, pos, _rope_tables(), waq, wakv, wape, q_norm.reshape(1, -1), kv_norm.reshape(1, -1), wuq, wuk, wuv)


def _attn_kernel(q_ref, k_ref, v_ref, o_ref, m_sc, l_sc, acc_sc):
    qi = pl.program_id(2)
    nrep = TQ // LANES
    m_sc[...] = jnp.full_like(m_sc, NEG_BIG)
    l_sc[...] = jnp.zeros_like(l_sc)
    acc_sc[...] = jnp.zeros_like(acc_sc)

    def step(kb, masked):
        start = pl.multiple_of(kb * TQ, TQ)
        v = v_ref[0, 0, pl.ds(start, TQ), :]
        scores = [lax.dot_general(q_ref[0, hh], k_ref[0, hh, pl.ds(start, TQ), :], (((1,), (1,)), ((), ())),
                                  preferred_element_type=F32) for hh in range(2)]
        for hh in range(2):
            s = scores[hh]
            if masked:
                row = lax.broadcasted_iota(jnp.int32, (TQ, TQ), 0)
                col = lax.broadcasted_iota(jnp.int32, (TQ, TQ), 1)
                s = jnp.where(col <= row, s, NEG_BIG)
            m_prev = m_sc[hh]
            m_new = jnp.maximum(m_prev, jnp.max(s, axis=-1, keepdims=True))
            p = jnp.exp2(s - jnp.concatenate([m_new] * nrep, axis=1))
            alpha = jnp.exp2(m_prev - m_new)
            psum = p[:, 0:LANES]
            for t in range(1, nrep):
                psum = psum + p[:, t * LANES:(t + 1) * LANES]
            l_sc[hh] = alpha * l_sc[hh] + psum
            acc_sc[hh] = alpha * acc_sc[hh] + _dot(p.astype(BF16), v)
            m_sc[hh] = m_new

    def two_blocks(j, _):
        step(2 * j, False)
        step(2 * j + 1, False)
        return 0

    lax.fori_loop(0, lax.shift_right_logical(qi, 1), two_blocks, 0)

    @pl.when((qi & 1) == 1)
    def _():
        step(qi - 1, False)

    step(qi, True)
    for hh in range(2):
        out = acc_sc[hh] / jnp.sum(l_sc[hh], axis=-1, keepdims=True)
        half = slice(hh * V_HEAD, (hh + 1) * V_HEAD)
        o_ref[0, :, half] = out[:, half].astype(BF16)


def _attention(q, k, v):
    b, _, s, _ = q.shape
    return pl.pallas_call(
        _attn_kernel,
        grid=(b, HEAD_PAIRS, s // TQ),
        in_specs=[pl.BlockSpec((1, 2, TQ, HEAD_BLOCK), lambda i, h, j: (i, h, j, 0)),
                  pl.BlockSpec((1, 2, s, HEAD_BLOCK), lambda i, h, j: (i, h, 0, 0)),
                  pl.BlockSpec((1, 1, s, LANES), lambda i, h, j: (i, h, 0, 0))],
        out_specs=pl.BlockSpec((1, TQ, LANES), lambda i, h, j: (i, j, h)),
        out_shape=jax.ShapeDtypeStruct((b, s, MLA_HEADS * V_HEAD), BF16),
        scratch_shapes=[pltpu.VMEM((2, TQ, LANES), F32)] * 3,
        compiler_params=_params(3),
        name="mla_attention",
    )(q, k, v)


def _out_proj_kernel(x_ref, a_ref, mod_ref, w_ref, lng_ref, lnb_ref, o_ref):
    y = _dot(a_ref[0], w_ref[...])
    g = mod_ref[0, 2:3, :]
    o_ref[0] = _resid_ln(x_ref[0], y, g, lng_ref[...], lnb_ref[...])


def _out_proj_sublayer(x, attn, mod, w_o, ln_g, ln_b):
    b, s, d = x.shape
    return pl.pallas_call(
        _out_proj_kernel,
        grid=(b, s // TM),
        in_specs=[pl.BlockSpec((1, TM, d), lambda i, j: (i, j, 0)),
                  pl.BlockSpec((1, TM, d), lambda i, j: (i, j, 0)),
                  pl.BlockSpec((1, 6, d), lambda i, j: (i, 0, 0)),
                  _full(w_o.shape), _full((1, d)), _full((1, d))],
        out_specs=pl.BlockSpec((1, TM, d), lambda i, j: (i, j, 0)),
        out_shape=jax.ShapeDtypeStruct(x.shape, F32),
        compiler_params=_params(2),
        name="mla_out_proj",
    )(x, attn, mod, w_o.astype(BF16), ln_g.reshape(1, d), ln_b.reshape(1, d))


def kernel(x, c, positions, mod_w, mod_b, ln_g, ln_b, pool_w, pool_scale, mla_w_a, mla_q_norm, mla_w_uq,
           mla_kv_norm, mla_w_ukv, mla_w_o, sc_w_in, sc_conv, sc_w_out, ffn_w_up, ffn_conv, ffn_conv_b, ffn_w_down):
    b = x.shape[0]
    mods = _modulation(c, mod_w, mod_b).reshape(DEPTH, b, 6, D_MODEL)
    for i in range(DEPTH):
        mod = mods[i]
        kind, j = i % 3, i // 3
        if kind == 0:
            x = _pool_sublayer(x, mod, pool_w[j], pool_scale[j], ln_g[i, 0], ln_b[i, 0])
        elif kind == 1:
            q, k, v = _mla_projections(x, mod, positions, mla_w_a[j], mla_q_norm[j], mla_w_uq[j],
                                       mla_kv_norm[j], mla_w_ukv[j])
            attn = _attention(q, k, v)
            x = _out_proj_sublayer(x, attn, mod, mla_w_o[j], ln_g[i, 0], ln_b[i, 0])
        else:
            x = _sconv_sublayer(x, mod, sc_w_in[j], sc_conv[j], sc_w_out[j], ln_g[i, 0], ln_b[i, 0])
        x = _ffn_sublayer(x, mod, ffn_w_up[i], ffn_conv[i], ffn_conv_b[i], ffn_w_down[i], ln_g[i, 1], ln_b[i, 1])
    return x
```

```python
import functools

import numpy as np
import jax
import jax.numpy as jnp
from jax import lax
from jax.experimental import pallas as pl
from jax.experimental.pallas import tpu as pltpu

D_MODEL = 1024
DEPTH = 4
POOL_WINDOWS = (2, 4, 8, 16)
POOL_GROUP = D_MODEL // len(POOL_WINDOWS)
POOL_HALO = 16
POOL_ROWS = 128
MLA_HEADS = 16
HEAD_PAIRS = MLA_HEADS // 2
QK_NOPE = 64
QK_ROPE = 32
V_HEAD = 64
Q_LORA = 768
KV_LORA = 256
ROPE_THETA = 10000.0
FFN_HIDDEN = 2816
DEEPNORM_ALPHA = (2 * DEPTH) ** 0.25
LN_EPS = 1e-5
RMS_EPS = 1e-6
SM_SCALE = (QK_NOPE + QK_ROPE) ** -0.5
LOG2E = 1.4426950408889634
NEG_BIG = -1e30

LANES = 128
SUBLANES = 8
HEAD_BLOCK = LANES
ROPE_LO = QK_NOPE
ROPE_HALF = QK_ROPE // 2

TM = 512
ROW_BLOCK = 64
FFN_CHUNK = 256
DOWN_GROUP = 4
UP_AHEAD = 2
TQ = 512
MOD_TN = 1536
VMEM_LIMIT = 56 * 1024 * 1024

BF16 = jnp.bfloat16
F32 = jnp.float32


def _silu(v):
    return v * (1.0 / (1.0 + jnp.exp(-v)))


def _dot(a, b):
    return jnp.dot(a, b, preferred_element_type=F32)


def _resid_ln(x, y, g, ln_g, ln_b):
    r = DEEPNORM_ALPHA * x + (1.0 + g) * y
    mu = jnp.mean(r, axis=-1, keepdims=True)
    d = r - mu
    var = jnp.mean(d * d, axis=-1, keepdims=True)
    return d * lax.rsqrt(var + LN_EPS) * ln_g + ln_b


def _params(n_grid):
    return pltpu.CompilerParams(dimension_semantics=("arbitrary",) * n_grid,
                                vmem_limit_bytes=VMEM_LIMIT)


def _full(shape):
    n = len(shape)
    return pl.BlockSpec(shape, lambda *_: (0,) * n)


def _mod_kernel(c_ref, w_ref, b_ref, o_ref):
    cond = _silu(c_ref[...]).astype(BF16)
    o_ref[0] = _dot(cond, w_ref[0].astype(BF16)) + b_ref[0]


def _modulation(c, mod_w, mod_b):
    b = c.shape[0]
    n = mod_w.shape[-1]
    return pl.pallas_call(
        _mod_kernel,
        grid=(DEPTH, n // MOD_TN),
        in_specs=[_full((b, D_MODEL)),
                  pl.BlockSpec((1, D_MODEL, MOD_TN), lambda i, j: (i, 0, j)),
                  pl.BlockSpec((1, 1, MOD_TN), lambda i, j: (i, 0, j))],
        out_specs=pl.BlockSpec((1, b, MOD_TN), lambda i, j: (i, 0, j)),
        out_shape=jax.ShapeDtypeStruct((DEPTH, b, n), F32),
        compiler_params=_params(2),
        name="modulation",
    )(c, mod_w, mod_b.reshape(DEPTH, 1, n))


def _mlp_kernel(kind, row0, fc, nchunks, x_ref, mod_ref, wup_ref, cw_ref, cb_ref, wdn_ref, lng_ref, lnb_ref,
                o_ref, ubuf, pbuf, hbuf, abuf, carry, acc):
    @pl.when(pl.program_id(1) == 0)
    def _():
        carry[...] = jnp.zeros_like(carry)

    x = x_ref[0]
    sh = mod_ref[0, row0:row0 + 1, :]
    sc = mod_ref[0, row0 + 1:row0 + 2, :]
    ubuf[...] = (x * (1.0 + sc) + sh).astype(BF16)

    def up(c, slot):
        hb = hbuf.at[slot]
        pb = pbuf.at[slot]
        hb[0:SUBLANES, :] = carry[c]
        parts = 2 if kind == "ffn" else 3
        n = wup_ref.shape[1] // parts
        for p in range(parts):
            res = _dot(ubuf[...], wup_ref[:, p * n + c * fc:p * n + (c + 1) * fc])
            if kind == "ffn":
                hb[SUBLANES:SUBLANES + TM, p * fc:(p + 1) * fc] = res
            else:
                pb[:, p * fc:(p + 1) * fc] = res
        if kind == "sconv":
            for r in range(0, TM, ROW_BLOCK):
                hb[SUBLANES + r:SUBLANES + r + ROW_BLOCK, :] = (
                    pb[r:r + ROW_BLOCK, fc:2 * fc] * pb[r:r + ROW_BLOCK, 2 * fc:3 * fc])
        carry[c] = hb[TM:TM + SUBLANES, :]

    def activate(c, slot):
        hb = hbuf.at[slot]
        pb = pbuf.at[slot]
        cw = cw_ref[c]
        cb = cb_ref[c]

        def conv3(r, cols):
            return (cw[2:3, cols] * hb[SUBLANES + r:SUBLANES + r + ROW_BLOCK, cols]
                    + cw[1:2, cols] * hb[SUBLANES - 1 + r:SUBLANES - 1 + r + ROW_BLOCK, cols]
                    + cw[0:1, cols] * hb[SUBLANES - 2 + r:SUBLANES - 2 + r + ROW_BLOCK, cols])

        for lo in range(0, fc, LANES):
            val = slice(lo, lo + LANES)
            gate = slice(fc + lo, fc + lo + LANES)
            for r in range(0, TM, ROW_BLOCK):
                if kind == "ffn":
                    a = _silu(conv3(r, gate) + cb[:, gate]) * (conv3(r, val) + cb[:, val])
                else:
                    a = pb[r:r + ROW_BLOCK, val] * conv3(r, val)
                abuf[r:r + ROW_BLOCK, c * fc + lo:c * fc + lo + LANES] = a.astype(BF16)

    def down(c0, c1):
        return _dot(abuf[:, c0 * fc:c1 * fc], wdn_ref[c0 * fc:c1 * fc, :])

    y = None
    for c in range(min(UP_AHEAD, nchunks)):
        up(c, c % (UP_AHEAD + 1))
    for c in range(nchunks):
        if c + UP_AHEAD < nchunks:
            up(c + UP_AHEAD, (c + UP_AHEAD) % (UP_AHEAD + 1))
        activate(c, c % (UP_AHEAD + 1))
        c0 = c - c % DOWN_GROUP
        if c + 1 == nchunks:
            y = down(c0, c + 1) if c0 == 0 else acc[...] + down(c0, c + 1)
        elif (c + 1) % DOWN_GROUP == 0:
            if c0 == 0:
                acc[...] = down(c0, c + 1)
            else:
                acc[...] += down(c0, c + 1)

    g = mod_ref[0, row0 + 2:row0 + 3, :]
    o_ref[0] = _resid_ln(x, y, g, lng_ref[...], lnb_ref[...])


def _mlp_sublayer(kind, x, mod, row0, wup, cw, cb, wdn, ln_g, ln_b):
    b, s, d = x.shape
    nchunks = cw.shape[0]
    fc = wdn.shape[0] // nchunks
    w = wup.shape[1] // nchunks
    wc = cw.shape[-1]
    kern = functools.partial(_mlp_kernel, kind, row0, fc, nchunks)
    pshape = (TM, w) if kind == "sconv" else (SUBLANES, LANES)
    return pl.pallas_call(
        kern,
        grid=(b, s // TM),
        in_specs=[pl.BlockSpec((1, TM, d), lambda i, j: (i, j, 0)),
                  pl.BlockSpec((1, 6, d), lambda i, j: (i, 0, 0)),
                  _full(wup.shape), _full(cw.shape), _full(cb.shape), _full(wdn.shape),
                  _full((1, d)), _full((1, d))],
        out_specs=pl.BlockSpec((1, TM, d), lambda i, j: (i, j, 0)),
        out_shape=jax.ShapeDtypeStruct(x.shape, F32),
        scratch_shapes=[pltpu.VMEM((TM, d), BF16),
                        pltpu.VMEM((UP_AHEAD + 1,) + pshape, F32),
                        pltpu.VMEM((UP_AHEAD + 1, TM + SUBLANES, wc), F32),
                        pltpu.VMEM((TM, nchunks * fc), BF16),
                        pltpu.VMEM((nchunks, SUBLANES, wc), F32),
                        pltpu.VMEM((TM, d), F32)],
        compiler_params=_params(2),
        name=kind + "_sublayer",
    )(x, mod, wup, cw, cb, wdn, ln_g.reshape(1, d), ln_b.reshape(1, d))


def _chunk_cols(w, parts, fc):
    lead = w.shape[:-1]
    n = w.shape[-1] // parts
    w = w.reshape(lead + (parts, n // fc, fc))
    w = jnp.moveaxis(w, -2, 0)
    return w.reshape((n // fc,) + lead + (parts * fc,))


def _ffn_sublayer(x, mod, w_up, conv_w, conv_b, w_down, ln_g, ln_b):
    fc = FFN_CHUNK
    wup = w_up.astype(BF16)
    cw = _chunk_cols(conv_w, 2, fc)
    cb = _chunk_cols(conv_b.reshape(1, -1), 2, fc)
    wdn = w_down.astype(BF16)
    return _mlp_sublayer("ffn", x, mod, 3, wup, cw, cb, wdn, ln_g, ln_b)


def _sconv_sublayer(x, mod, w_in, conv_w, w_out, ln_g, ln_b):
    fc = FFN_CHUNK
    wup = w_in.astype(BF16)
    cw = _chunk_cols(conv_w, 1, fc)
    cb = jnp.zeros((D_MODEL // fc, 1, fc), F32)
    wdn = w_out.astype(BF16)
    return _mlp_sublayer("sconv", x, mod, 0, wup, cw, cb, wdn, ln_g, ln_b)


def _pool_kernel(x_ref, mod_ref, w_ref, scale_ref, lng_ref, lnb_ref, o_ref, ubuf, pbuf, ybuf):
    j = pl.program_id(1)

    @pl.when(j == 0)
    def _():
        ubuf[0:POOL_HALO, :] = jnp.zeros((POOL_HALO, D_MODEL), F32)

    @pl.when(j > 0)
    def _():
        ubuf[0:POOL_HALO, :] = ubuf[TM:TM + POOL_HALO, :]

    x = x_ref[0]
    sh = mod_ref[0, 0:1, :]
    sc = mod_ref[0, 1:2, :]
    ubuf[POOL_HALO:POOL_HALO + TM, :] = x * (1.0 + sc) + sh
    for gi, win in enumerate(POOL_WINDOWS):
        lo = gi * POOL_GROUP
        cols = slice(lo, lo + POOL_GROUP)
        for r in range(0, TM, POOL_ROWS):
            ext = ubuf[r:r + POOL_HALO + POOL_ROWS, cols]
            tot = ext
            shift = 1
            while shift < win:
                tot = tot + pltpu.roll(tot, shift, 0)
                shift *= 2
            t = j * TM + r + lax.broadcasted_iota(jnp.int32, (POOL_ROWS, 1), 0)
            count = jnp.minimum(t + 1, win).astype(F32)
            pooled = tot[POOL_HALO:, :] / count - ext[POOL_HALO:, :]
            pbuf[r:r + POOL_ROWS, cols] = pooled.astype(BF16)
        ybuf[:, cols] = _dot(pbuf[:, cols], w_ref[gi])
    y = ybuf[...] * scale_ref[...]
    g = mod_ref[0, 2:3, :]
    o_ref[0] = _resid_ln(x, y, g, lng_ref[...], lnb_ref[...])


def _pool_sublayer(x, mod, w_groups, scale, ln_g, ln_b):
    b, s, d = x.shape
    return pl.pallas_call(
        _pool_kernel,
        grid=(b, s // TM),
        in_specs=[pl.BlockSpec((1, TM, d), lambda i, j: (i, j, 0)),
                  pl.BlockSpec((1, 6, d), lambda i, j: (i, 0, 0)),
                  _full(w_groups.shape), _full((1, d)), _full((1, d)), _full((1, d))],
        out_specs=pl.BlockSpec((1, TM, d), lambda i, j: (i, j, 0)),
        out_shape=jax.ShapeDtypeStruct(x.shape, F32),
        scratch_shapes=[pltpu.VMEM((TM + POOL_HALO, d), F32),
                        pltpu.VMEM((TM, d), BF16),
                        pltpu.VMEM((TM, d), F32)],
        compiler_params=_params(2),
        name="pool_sublayer",
    )(x, mod, w_groups.astype(BF16), scale.reshape(1, d), ln_g.reshape(1, d), ln_b.reshape(1, d))


def _rope_tables():
    inv_freq = ROPE_THETA ** (-np.arange(0, QK_ROPE, 2, dtype=np.float32) / QK_ROPE)
    tab = np.zeros((SUBLANES, LANES), np.float32)
    a, m, e = ROPE_LO, ROPE_LO + ROPE_HALF, ROPE_LO + QK_ROPE
    tab[0, a:m] = inv_freq
    tab[0, m:e] = inv_freq
    tab[1, :a] = 1.0
    tab[2, a:e] = 1.0
    tab[3, a:m] = -1.0
    tab[4, m:e] = 1.0
    return jnp.asarray(tab)


def _rms(v, gain):
    return v * lax.rsqrt(jnp.mean(v * v, axis=-1, keepdims=True) + RMS_EPS) * gain


def _mla_proj_kernel(x_ref, mod_ref, pos_ref, tab_ref, waq_ref, wakv_ref, wape_ref, qn_ref, kvn_ref,
                     wuq_ref, wuk_ref, wuv_ref, q_ref, k_ref, v_ref):
    x = x_ref[0]
    sh = mod_ref[0, 0:1, :]
    sc = mod_ref[0, 1:2, :]
    u = (x * (1.0 + sc) + sh).astype(BF16)
    cq = _rms(_dot(u, waq_ref[...]), qn_ref[...]).astype(BF16)
    ckv = _rms(_dot(u, wakv_ref[...]), kvn_ref[...]).astype(BF16)
    ape = _dot(u, wape_ref[...])

    tab = tab_ref[...]
    ang = pos_ref[0] * tab[0:1, :]
    cs = jnp.cos(ang)
    sn = jnp.sin(ang)
    keep = tab[1:2, :] + tab[2:3, :] * cs
    s_lo = tab[3:4, :] * sn
    s_hi = tab[4:5, :] * sn

    def rope(v):
        return (v * keep + pltpu.roll(v, LANES - ROPE_HALF, 1) * s_lo
                + pltpu.roll(v, ROPE_HALF, 1) * s_hi)

    kpe = rope(ape)
    for hp in range(HEAD_PAIRS):
        q2 = _dot(cq, wuq_ref[hp])
        k2 = _dot(ckv, wuk_ref[hp])
        for hh in range(2):
            cols = slice(hh * HEAD_BLOCK, (hh + 1) * HEAD_BLOCK)
            q_ref[0, 2 * hp + hh] = (rope(q2[:, cols]) * (SM_SCALE * LOG2E)).astype(BF16)
            k_ref[0, 2 * hp + hh] = (k2[:, cols] + kpe).astype(BF16)
    v = _dot(ckv, wuv_ref[...])
    for hp in range(HEAD_PAIRS):
        v_ref[0, hp] = v[:, hp * LANES:(hp + 1) * LANES].astype(BF16)


def _mla_projections(x, mod, positions, w_a, q_norm, w_uq, kv_norm, w_ukv):
    b, s, d = x.shape
    w_a = w_a.astype(BF16)
    waq = w_a[:, :Q_LORA]
    wakv = w_a[:, Q_LORA:Q_LORA + KV_LORA]
    wape = jnp.zeros((d, HEAD_BLOCK), BF16).at[:, ROPE_LO:ROPE_LO + QK_ROPE].set(w_a[:, Q_LORA + KV_LORA:])
    wuq = w_uq.astype(BF16).reshape(Q_LORA, MLA_HEADS, QK_NOPE + QK_ROPE)
    wuq = jnp.pad(wuq, ((0, 0), (0, 0), (0, HEAD_BLOCK - QK_NOPE - QK_ROPE)))
    wuq = wuq.reshape(Q_LORA, HEAD_PAIRS, 2 * HEAD_BLOCK).transpose(1, 0, 2)
    wukv = w_ukv.astype(BF16).reshape(KV_LORA, MLA_HEADS, QK_NOPE + V_HEAD)
    wuk = jnp.pad(wukv[:, :, :QK_NOPE], ((0, 0), (0, 0), (0, HEAD_BLOCK - QK_NOPE)))
    wuk = wuk.reshape(KV_LORA, HEAD_PAIRS, 2 * HEAD_BLOCK).transpose(1, 0, 2)
    wuv = wukv[:, :, QK_NOPE:].reshape(KV_LORA, MLA_HEADS * V_HEAD)
    pos = positions.astype(F32).reshape(b, s, 1)
    head_shape = jax.ShapeDtypeStruct((b, MLA_HEADS, s, HEAD_BLOCK), BF16)
    return pl.pallas_call(
        _mla_proj_kernel,
        grid=(b, s // TM),
        in_specs=[pl.BlockSpec((1, TM, d), lambda i, j: (i, j, 0)),
                  pl.BlockSpec((1, 6, d), lambda i, j: (i, 0, 0)),
                  pl.BlockSpec((1, TM, 1), lambda i, j: (i, j, 0)),
                  _full((SUBLANES, LANES)),
                  _full(waq.shape), _full(wakv.shape), _full(wape.shape),
                  _full((1, Q_LORA)), _full((1, KV_LORA)),
                  _full(wuq.shape), _full(wuk.shape), _full(wuv.shape)],
        out_specs=[pl.BlockSpec((1, MLA_HEADS, TM, HEAD_BLOCK), lambda i, j: (i, 0, j, 0)),
                   pl.BlockSpec((1, MLA_HEADS, TM, HEAD_BLOCK), lambda i, j: (i, 0, j, 0)),
                   pl.BlockSpec((1, HEAD_PAIRS, TM, LANES), lambda i, j: (i, 0, j, 0))],
        out_shape=[head_shape, head_shape,
                   jax.ShapeDtypeStruct((b, HEAD_PAIRS, s, LANES), BF16)],
        compiler_params=_params(2),
        name="mla_projections",
    )(x, mod, pos, _rope_tables(), waq, wakv, wape, q_norm.reshape(1, -1), kv_norm.reshape(1, -1), wuq, wuk, wuv)


def _attn_kernel(q_ref, k_ref, v_ref, o_ref, m_sc, l_sc, acc_sc):
    qi = pl.program_id(2)
    nrep = TQ // LANES
    m_sc[...] = jnp.full_like(m_sc, NEG_BIG)
    l_sc[...] = jnp.zeros_like(l_sc)
    acc_sc[...] = jnp.zeros_like(acc_sc)

    def block(q0, qn, start, kn, mask_shift):
        rows = slice(q0, q0 + qn)
        v = v_ref[0, 0, pl.ds(start, kn), :]
        scores = [lax.dot_general(q_ref[0, hh, rows, :], k_ref[0, hh, pl.ds(start, kn), :], (((1,), (1,)), ((), ())),
                                  preferred_element_type=F32) for hh in range(2)]
        for hh in range(2):
            s = scores[hh]
            if mask_shift is not None:
                row = lax.broadcasted_iota(jnp.int32, (qn, kn), 0)
                col = lax.broadcasted_iota(jnp.int32, (qn, kn), 1)
                s = jnp.where(col <= row + mask_shift, s, NEG_BIG)
            m_prev = m_sc[hh, rows, :]
            m_new = jnp.maximum(m_prev, jnp.max(s, axis=-1, keepdims=True))
            p = jnp.exp2(s - jnp.concatenate([m_new] * (kn // LANES), axis=1))
            alpha = jnp.exp2(m_prev - m_new)
            psum = p[:, 0:LANES]
            for t in range(1, kn // LANES):
                psum = psum + p[:, t * LANES:(t + 1) * LANES]
            l_sc[hh, rows, :] = alpha * l_sc[hh, rows, :] + psum
            acc_sc[hh, rows, :] = alpha * acc_sc[hh, rows, :] + _dot(p.astype(BF16), v)
            m_sc[hh, rows, :] = m_new

    def step(kb):
        block(0, TQ, pl.multiple_of(kb * TQ, TQ), TQ, None)

    def two_blocks(j, _):
        step(2 * j)
        step(2 * j + 1)
        return 0

    lax.fori_loop(0, lax.shift_right_logical(qi, 1), two_blocks, 0)

    @pl.when((qi & 1) == 1)
    def _():
        step(qi - 1)

    diag = pl.multiple_of(qi * TQ, TQ)
    block(0, TQ // 2, diag, TQ // 2, 0)
    block(TQ // 2, TQ // 2, diag, TQ, TQ // 2)
    for hh in range(2):
        out = acc_sc[hh] / jnp.sum(l_sc[hh], axis=-1, keepdims=True)
        half = slice(hh * V_HEAD, (hh + 1) * V_HEAD)
        o_ref[0, :, half] = out[:, half].astype(BF16)


def _attention(q, k, v):
    b, _, s, _ = q.shape
    return pl.pallas_call(
        _attn_kernel,
        grid=(b, HEAD_PAIRS, s // TQ),
        in_specs=[pl.BlockSpec((1, 2, TQ, HEAD_BLOCK), lambda i, h, j: (i, h, j, 0)),
                  pl.BlockSpec((1, 2, s, HEAD_BLOCK), lambda i, h, j: (i, h, 0, 0)),
                  pl.BlockSpec((1, 1, s, LANES), lambda i, h, j: (i, h, 0, 0))],
        out_specs=pl.BlockSpec((1, TQ, LANES), lambda i, h, j: (i, j, h)),
        out_shape=jax.ShapeDtypeStruct((b, s, MLA_HEADS * V_HEAD), BF16),
        scratch_shapes=[pltpu.VMEM((2, TQ, LANES), F32)] * 3,
        compiler_params=_params(3),
        name="mla_attention",
    )(q, k, v)


def _out_proj_kernel(x_ref, a_ref, mod_ref, w_ref, lng_ref, lnb_ref, o_ref):
    y = _dot(a_ref[0], w_ref[...])
    g = mod_ref[0, 2:3, :]
    o_ref[0] = _resid_ln(x_ref[0], y, g, lng_ref[...], lnb_ref[...])


def _out_proj_sublayer(x, attn, mod, w_o, ln_g, ln_b):
    b, s, d = x.shape
    return pl.pallas_call(
        _out_proj_kernel,
        grid=(b, s // TM),
        in_specs=[pl.BlockSpec((1, TM, d), lambda i, j: (i, j, 0)),
                  pl.BlockSpec((1, TM, d), lambda i, j: (i, j, 0)),
                  pl.BlockSpec((1, 6, d), lambda i, j: (i, 0, 0)),
                  _full(w_o.shape), _full((1, d)), _full((1, d))],
        out_specs=pl.BlockSpec((1, TM, d), lambda i, j: (i, j, 0)),
        out_shape=jax.ShapeDtypeStruct(x.shape, F32),
        compiler_params=_params(2),
        name="mla_out_proj",
    )(x, attn, mod, w_o.astype(BF16), ln_g.reshape(1, d), ln_b.reshape(1, d))


def kernel(x, c, positions, mod_w, mod_b, ln_g, ln_b, pool_w, pool_scale, mla_w_a, mla_q_norm, mla_w_uq,
           mla_kv_norm, mla_w_ukv, mla_w_o, sc_w_in, sc_conv, sc_w_out, ffn_w_up, ffn_conv, ffn_conv_b, ffn_w_down):
    b = x.shape[0]
    mods = _modulation(c, mod_w, mod_b).reshape(DEPTH, b, 6, D_MODEL)
    for i in range(DEPTH):
        mod = mods[i]
        kind, j = i % 3, i // 3
        if kind == 0:
            x = _pool_sublayer(x, mod, pool_w[j], pool_scale[j], ln_g[i, 0], ln_b[i, 0])
        elif kind == 1:
            q, k, v = _mla_projections(x, mod, positions, mla_w_a[j], mla_q_norm[j], mla_w_uq[j],
                                       mla_kv_norm[j], mla_w_ukv[j])
            attn = _attention(q, k, v)
            x = _out_proj_sublayer(x, attn, mod, mla_w_o[j], ln_g[i, 0], ln_b[i, 0])
        else:
            x = _sconv_sublayer(x, mod, sc_w_in[j], sc_conv[j], sc_w_out[j], ln_g[i, 0], ln_b[i, 0])
        x = _ffn_sublayer(x, mod, ffn_w_up[i], ffn_conv[i], ffn_conv_b[i], ffn_w_down[i], ln_g[i, 1], ln_b[i, 1])
    return x
```
